```python
import math
import jax
import jax.numpy as jnp
from jax import lax
import numpy as np

D_MODEL = 2048
BATCH = 8
SEQ = 4096
DEPTH = 4

HEAD_DIM = 128
MIX_WIDTH = D_MODEL
N_HEADS_TOTAL = MIX_WIDTH // HEAD_DIM
N_HEADS_SB = N_HEADS_TOTAL // 2
N_HEADS_DIL = N_HEADS_TOTAL - N_HEADS_SB
SB_WIDTH = N_HEADS_SB * HEAD_DIM
DIL_WIDTH = N_HEADS_DIL * HEAD_DIM
IN_WIDTH = 3 * (SB_WIDTH + DIL_WIDTH)
IN_SPLITS = (SB_WIDTH, 2 * SB_WIDTH, 3 * SB_WIDTH, 3 * SB_WIDTH + DIL_WIDTH, 3 * SB_WIDTH + 2 * DIL_WIDTH)
D_FF = -(-8 * D_MODEL // (3 * 256)) * 256
PLI_DIM = 256
BLOCK = 128
DIL_GROUPS = ((128, 1), (512, 4), (2048, 16))
NUM_BUCKETS = 32
MAX_DISTANCE = 2048
RMS_EPS = 1e-6

kernel_name = "hybrid_stickbreak_dilated_sandwich_trunk"


def rms_norm(x, gain):
    xf = x.astype(jnp.float32)
    y = xf * lax.rsqrt(jnp.mean(xf * xf, axis=-1, keepdims=True) + RMS_EPS)
    return (y * gain.astype(jnp.float32)).astype(x.dtype)


def t5_bucket(dist):
    max_exact = NUM_BUCKETS // 2
    d = jnp.maximum(dist, 1).astype(jnp.float32)
    large = max_exact + (jnp.log(d / max_exact) / math.log(MAX_DISTANCE / max_exact)
                         * (NUM_BUCKETS - max_exact)).astype(jnp.int32)
    large = jnp.minimum(large, NUM_BUCKETS - 1)
    return jnp.where(dist < max_exact, dist, large)


def stick_breaking_attention(q, k, v):
    b, s, h, e = q.shape
    nblk = s // BLOCK
    scale = 1.0 / math.sqrt(e)
    kt = k.transpose(0, 2, 1, 3)
    vt = v.transpose(0, 2, 1, 3).astype(jnp.float32)
    qb = q.reshape(b, nblk, BLOCK, h, e).transpose(1, 0, 3, 2, 4)
    starts = jnp.arange(nblk, dtype=jnp.int32) * BLOCK
    key_pos = jnp.arange(s, dtype=jnp.int32)

    def one_block(args):
        qi, t0 = args
        z = jnp.einsum('bhqe,bhke->bhqk', qi, kt).astype(jnp.float32) * scale
        q_pos = t0 + jnp.arange(BLOCK, dtype=jnp.int32)
        earlier = key_pos[None, :] < q_pos[:, None]
        log_keep = jnp.where(earlier, jax.nn.log_sigmoid(-z), 0.0)
        later = lax.cumsum(log_keep, axis=3, reverse=True) - log_keep
        a = jnp.where(earlier, jnp.exp(jax.nn.log_sigmoid(z) + later), 0.0)
        return jnp.einsum('bhqk,bhke->bhqe', a, vt)

    o = lax.map(one_block, (qb, starts))
    return o.transpose(1, 0, 3, 2, 4).reshape(b, s, h, e).astype(q.dtype)


def dilated_branch(q, k, v, rel_bias, window, dilation):
    b, s, h, e = q.shape
    span = dilation * BLOCK
    s_pad = -(-s // span) * span
    nb = s_pad // span
    n_back = window // dilation
    scale = 1.0 / math.sqrt(e)

    def to_sub(t):
        t = jnp.pad(t, ((0, 0), (0, s_pad - s), (0, 0), (0, 0)))
        return t.reshape(b, nb, BLOCK, dilation, h, e).transpose(0, 3, 4, 1, 2, 5)

    def with_prev(t):
        prev = jnp.pad(t[:, :, :, :-1], ((0, 0), (0, 0), (0, 0), (1, 0), (0, 0), (0, 0)))
        return jnp.concatenate([prev, t], axis=4)

    qs = to_sub(q)
    kk = with_prev(to_sub(k))
    vv = with_prev(to_sub(v)).astype(jnp.float32)
    logits = jnp.einsum('brhnqe,brhnke->brhnqk', qs, kk).astype(jnp.float32) * scale

    qi = jnp.arange(BLOCK, dtype=jnp.int32)[:, None]
    ki = jnp.arange(2 * BLOCK, dtype=jnp.int32)[None, :]
    rel = BLOCK + qi - ki
    band = (rel >= 0) & (rel <= n_back)
    bias = rel_bias.astype(jnp.float32)[t5_bucket(jnp.maximum(rel, 0) * dilation)]
    bias = bias.transpose(2, 0, 1)[:, None]
    has_prev = (jnp.arange(nb)[:, None, None] > 0) | (ki[None] >= BLOCK)
    valid = band[None] & has_prev

    logits = jnp.where(valid, logits + bias, -jnp.inf)
    m = jnp.max(logits, axis=-1, keepdims=True)
    p = jnp.exp(logits - m)
    denom = jnp.sum(p, axis=-1, keepdims=True)
    o = jnp.einsum('brhnqk,brhnke->brhnqe', p, vv) / denom
    lse = m + jnp.log(denom)

    def from_sub(t):
        return t.transpose(0, 3, 4, 1, 2, 5).reshape(b, s_pad, h, t.shape[-1])[:, :s]

    return from_sub(o), from_sub(lse)[..., 0]


def dilated_mixture(q, k, v, rel_bias):
    outs, lses = [], []
    for window, dilation in DIL_GROUPS:
        o, l = dilated_branch(q, k, v, rel_bias, window, dilation)
        outs.append(o)
        lses.append(l)
    w = jax.nn.softmax(jnp.stack(lses, axis=0), axis=0)
    o = jnp.sum(w[..., None] * jnp.stack(outs, axis=0), axis=0)
    return o.astype(q.dtype)


def setup_inputs(seed: int = 0) -> dict:
    key = jax.random.key(seed)
    ks = jax.random.split(key, 15)

    def normal(k, shape, scale):
        return jax.random.normal(k, shape, jnp.float32) * scale

    def gain(k, shape):
        return 1.0 + normal(k, shape, 0.01)

    return {
        "x": normal(ks[0], (BATCH, SEQ, D_MODEL), 1.0),
        "p": normal(ks[1], (DEPTH, BATCH, SEQ, PLI_DIM), 1.0),
        "ln_mix_pre": gain(ks[2], (DEPTH, D_MODEL)),
        "w_in": normal(ks[3], (DEPTH, D_MODEL, IN_WIDTH), D_MODEL ** -0.5),
        "ln_head": gain(ks[4], (DEPTH, MIX_WIDTH)),
        "w_out": normal(ks[5], (DEPTH, MIX_WIDTH, D_MODEL), MIX_WIDTH ** -0.5),
        "ln_mix_post": gain(ks[6], (DEPTH, D_MODEL)),
        "rel_bias": normal(ks[7], (NUM_BUCKETS, N_HEADS_DIL), 0.5),
        "ln_ffn_pre": gain(ks[8], (DEPTH, D_MODEL)),
        "w_gate_up": normal(ks[9], (DEPTH, D_MODEL, 2 * D_FF), D_MODEL ** -0.5),
        "w_down": normal(ks[10], (DEPTH, D_FF, D_MODEL), D_FF ** -0.5),
        "ln_ffn_post": gain(ks[11], (DEPTH, D_MODEL)),
        "ln_pli": gain(ks[12], (DEPTH, D_MODEL)),
        "w_pli_gate": normal(ks[13], (DEPTH, D_MODEL, D_MODEL), D_MODEL ** -0.5),
        "w_pli_proj": normal(ks[14], (DEPTH, PLI_DIM, D_MODEL), PLI_DIM ** -0.5),
    }


def reference(x, p, ln_mix_pre, w_in, ln_head, w_out, ln_mix_post, rel_bias,
              ln_ffn_pre, w_gate_up, w_down, ln_ffn_post, ln_pli, w_pli_gate, w_pli_proj):
    b, s, _ = x.shape
    for i in range(DEPTH):
        h = rms_norm(x, ln_mix_pre[i])
        proj = h @ w_in[i]
        q_sb, k_sb, v_sb, q_dl, k_dl, v_dl = jnp.split(proj, list(IN_SPLITS), axis=-1)
        sb_shape = (b, s, N_HEADS_SB, HEAD_DIM)
        dl_shape = (b, s, N_HEADS_DIL, HEAD_DIM)
        o_sb = stick_breaking_attention(q_sb.reshape(sb_shape), k_sb.reshape(sb_shape), v_sb.reshape(sb_shape))
        o_dl = dilated_mixture(q_dl.reshape(dl_shape), k_dl.reshape(dl_shape), v_dl.reshape(dl_shape), rel_bias)
        o = jnp.concatenate([o_sb, o_dl], axis=2)
        o = rms_norm(o, ln_head[i].reshape(N_HEADS_TOTAL, HEAD_DIM)).reshape(b, s, MIX_WIDTH)
        x = x + rms_norm(o @ w_out[i], ln_mix_post[i])
        h = rms_norm(x, ln_ffn_pre[i])
        g, u = jnp.split(h @ w_gate_up[i], 2, axis=-1)
        f = (jax.nn.silu(g) * u) @ w_down[i]
        x = x + rms_norm(f, ln_ffn_post[i])
        gate = jax.nn.sigmoid(rms_norm(x, ln_pli[i]) @ w_pli_gate[i])
        x = x + gate * (p[i] @ w_pli_proj[i])
    return x
```

```python
import functools
import math

import jax
import jax.numpy as jnp
import numpy as np
from jax import lax
from jax.experimental import pallas as pl
from jax.experimental.pallas import tpu as pltpu

HEAD_DIM = 128
N_HEADS_SB = 8
N_HEADS_DIL = 8
N_HEADS = N_HEADS_SB + N_HEADS_DIL
BLOCK = 128
DIL_GROUPS = ((128, 1), (512, 4), (2048, 16))
NUM_BUCKETS = 32
MAX_DISTANCE = 2048
RMS_EPS = 1e-6
MASK_VALUE = -1e30

VMEM_LIMIT_BYTES = 56 * 1024 * 1024

F32 = jnp.float32
BF16 = jnp.bfloat16


def _params(semantics):
    return pltpu.CompilerParams(dimension_semantics=semantics, vmem_limit_bytes=VMEM_LIMIT_BYTES)


def _rms(x, gain):
    ms = jnp.mean(x * x, axis=-1, keepdims=True)
    return x * lax.rsqrt(ms + RMS_EPS) * gain


def _qkv_kernel(x_ref, g_ref, w_ref, o_ref, h_ref):
    @pl.when(pl.program_id(1) == 0)
    def _():
        h_ref[...] = _rms(x_ref[...], g_ref[...]).astype(BF16)

    r = jnp.dot(h_ref[...], w_ref[...], preferred_element_type=F32)
    for hh in range(o_ref.shape[0]):
        o_ref[hh] = r[:, hh * HEAD_DIM:(hh + 1) * HEAD_DIM].astype(BF16)


def _qkv_proj(x2, gain, w, *, tm):
    n, d = x2.shape
    heads_per_step = 8
    tn = heads_per_step * HEAD_DIM
    n_col = w.shape[1] // tn
    return pl.pallas_call(
        _qkv_kernel,
        out_shape=jax.ShapeDtypeStruct((n_col * heads_per_step, n, HEAD_DIM), BF16),
        grid=(n // tm, n_col),
        in_specs=[
            pl.BlockSpec((tm, d), lambda i, j: (i, 0)),
            pl.BlockSpec((1, d), lambda i, j: (0, 0)),
            pl.BlockSpec((d, tn), lambda i, j: (0, j)),
        ],
        out_specs=pl.BlockSpec((heads_per_step, tm, HEAD_DIM), lambda i, j: (j, i, 0)),
        scratch_shapes=[pltpu.VMEM((tm, d), BF16)],
        compiler_params=_params(("parallel", "arbitrary")),
        name="qkv_proj",
    )(x2, gain, w)


def _sb_kernel(q_ref, k_ref, v_ref, o_ref, u_ref, c_ref, acc_ref, *, tq, scale):
    i = pl.program_id(2)
    row = lax.broadcasted_iota(jnp.int32, (tq, tq), 0)
    col = lax.broadcasted_iota(jnp.int32, (tq, tq), 1)
    u_ref[...] = jnp.where(row > col, 1.0, 0.0).astype(BF16)
    q = q_ref[...]

    def tile(start, masked):
        k = k_ref[pl.ds(start, tq), :]
        v = v_ref[pl.ds(start, tq), :]
        z = lax.dot_general(q, k, (((1,), (1,)), ((), ())), preferred_element_type=F32) * scale
        sp = jnp.maximum(z, 0.0) + jnp.log(1.0 + jnp.exp(-jnp.abs(z)))
        if masked:
            earlier = col < row
            sp = jnp.where(earlier, sp, 0.0)
        hi = sp.astype(BF16)
        lo = (sp - hi.astype(F32)).astype(BF16)
        u = u_ref[...]
        suffix = (jnp.dot(hi, u, preferred_element_type=F32)
                  + jnp.dot(lo, u, preferred_element_type=F32))
        c = c_ref[...]
        a = jnp.exp(z - sp - suffix - c)
        if masked:
            a = jnp.where(earlier, a, 0.0)
        acc_ref[...] += jnp.dot(a.astype(BF16), v, preferred_element_type=F32)
        c_ref[...] = c + suffix[:, 0:1] + sp[:, 0:1]

    c_ref[...] = jnp.zeros_like(c_ref)
    acc_ref[...] = jnp.zeros_like(acc_ref)
    tile(pl.multiple_of(i * tq, tq), True)

    def body(s, carry):
        tile(pl.multiple_of((i - 1 - s) * tq, tq), False)
        return carry

    lax.fori_loop(0, i, body, 0)
    o_ref[...] = acc_ref[...]


def _sb_attention(qkv, b, s, *, tq):
    kern = functools.partial(_sb_kernel, tq=tq, scale=1.0 / math.sqrt(HEAD_DIM))
    return pl.pallas_call(
        kern,
        out_shape=jax.ShapeDtypeStruct((N_HEADS_SB, b, s, HEAD_DIM), F32),
        grid=(b, N_HEADS_SB, s // tq),
        in_specs=[
            pl.BlockSpec((None, None, tq, HEAD_DIM), lambda bi, h, i: (h, bi, i, 0)),
            pl.BlockSpec((None, None, s, HEAD_DIM), lambda bi, h, i: (N_HEADS_SB + h, bi, 0, 0)),
            pl.BlockSpec((None, None, s, HEAD_DIM), lambda bi, h, i: (2 * N_HEADS_SB + h, bi, 0, 0)),
        ],
        out_specs=pl.BlockSpec((None, None, tq, HEAD_DIM), lambda bi, h, i: (h, bi, i, 0)),
        scratch_shapes=[
            pltpu.VMEM((tq, tq), BF16),
            pltpu.VMEM((tq, 1), F32),
            pltpu.VMEM((tq, HEAD_DIM), F32),
        ],
        compiler_params=_params(("parallel", "parallel", "arbitrary")),
        name="sb_attn",
    )(qkv, qkv, qkv)


def _t5_bucket_np(dist):
    max_exact = NUM_BUCKETS // 2
    d = np.maximum(dist, 1).astype(np.float64)
    large = max_exact + (np.log(d / max_exact) / math.log(MAX_DISTANCE / max_exact)
                         * (NUM_BUCKETS - max_exact)).astype(np.int64)
    large = np.minimum(large, NUM_BUCKETS - 1)
    return np.where(dist < max_exact, dist, large)


def _bias_tables(rel_bias):
    qi = np.arange(BLOCK)[:, None]
    ki = np.arange(2 * BLOCK)[None, :]
    rel = BLOCK + qi - ki
    band = (rel >= 0) & (rel <= BLOCK)
    tables = []
    for _, dilation in DIL_GROUPS:
        bucket = _t5_bucket_np(np.maximum(rel, 0) * dilation)
        bias = rel_bias.astype(F32)[bucket]
        bias = jnp.where(band[:, :, None], bias, MASK_VALUE)
        tables.append(bias.transpose(2, 0, 1))
    return jnp.stack(tables, axis=0)


def _dil_kernel(*refs, scale):
    tbl_ref = refs[0]
    in_refs = refs[1:10]
    out_refs = refs[10:16]
    for g, (_, dilation) in enumerate(DIL_GROUPS):
        q_ref, k_ref, v_ref = in_refs[3 * g:3 * g + 3]
        o_ref, l_ref = out_refs[2 * g:2 * g + 2]
        nb = q_ref.shape[0]

        def tile(n, r, has_prev, g=g, q_ref=q_ref, k_ref=k_ref, v_ref=v_ref, o_ref=o_ref, l_ref=l_ref):
            lanes = slice(r * HEAD_DIM, (r + 1) * HEAD_DIM)
            q = q_ref[n, :, lanes]
            k1 = k_ref[n, :, lanes]
            v1 = v_ref[n, :, lanes]
            dn = (((1,), (1,)), ((), ()))
            s1 = lax.dot_general(q, k1, dn, preferred_element_type=F32) * scale + tbl_ref[g, :, BLOCK:]
            m = jnp.max(s1, axis=-1, keepdims=True)
            if has_prev:
                k0 = k_ref[n - 1, :, lanes]
                v0 = v_ref[n - 1, :, lanes]
                s0 = lax.dot_general(q, k0, dn, preferred_element_type=F32) * scale + tbl_ref[g, :, :BLOCK]
                m = jnp.maximum(m, jnp.max(s0, axis=-1, keepdims=True))
            p1 = jnp.exp(s1 - m)
            denom = jnp.sum(p1, axis=-1, keepdims=True)
            acc = jnp.dot(p1.astype(BF16), v1, preferred_element_type=F32)
            if has_prev:
                p0 = jnp.exp(s0 - m)
                denom = denom + jnp.sum(p0, axis=-1, keepdims=True)
                acc = acc + jnp.dot(p0.astype(BF16), v0, preferred_element_type=F32)
            o_ref[n, :, lanes] = acc / denom
            l_ref[n, :, lanes] = jnp.broadcast_to(m + jnp.log(denom), (BLOCK, HEAD_DIM))

        for r in range(dilation):
            tile(0, r, False)

        def body(n, carry, tile=tile, dilation=dilation):
            for r in range(dilation):
                tile(n, r, True)
            return carry

        lax.fori_loop(1, nb, body, 0)


def _dil_attention(qkv, tables, b, s):
    views, in_specs, out_shapes, out_specs = [], [], [], []
    for _, d in DIL_GROUPS:
        nb = s // (BLOCK * d)
        view = qkv.reshape(qkv.shape[0], b, nb, BLOCK, d * HEAD_DIM)
        for part in range(3):
            base = 3 * N_HEADS_SB + part * N_HEADS_DIL
            views.append(view)
            in_specs.append(pl.BlockSpec((None, None, nb, BLOCK, d * HEAD_DIM),
                                         lambda bi, h, base=base: (base + h, bi, 0, 0, 0)))
        for _ in range(2):
            out_shapes.append(jax.ShapeDtypeStruct((N_HEADS_DIL, b, nb, BLOCK, d * HEAD_DIM), F32))
            out_specs.append(pl.BlockSpec((None, None, nb, BLOCK, d * HEAD_DIM),
                                          lambda bi, h: (h, bi, 0, 0, 0)))
    tbl_spec = pl.BlockSpec((len(DIL_GROUPS), None, BLOCK, 2 * BLOCK), lambda bi, h: (0, h, 0, 0))
    outs = pl.pallas_call(
        functools.partial(_dil_kernel, scale=1.0 / math.sqrt(HEAD_DIM)),
        out_shape=out_shapes,
        grid=(b, N_HEADS_DIL),
        in_specs=[tbl_spec] + in_specs,
        out_specs=out_specs,
        compiler_params=_params(("parallel", "parallel")),
        name="dil_attn",
    )(tables, *views)
    return [o.reshape(N_HEADS_DIL, b * s, HEAD_DIM) for o in outs]


def _attn_out_kernel(x_ref, osb_ref, o0_ref, l0_ref, o1_ref, l1_ref, o2_ref, l2_ref,
                     gh_ref, w_ref, gp_ref, out_ref, lhs_ref):
    def head_norm(o, h):
        lanes = slice(h * HEAD_DIM, (h + 1) * HEAD_DIM)
        lhs_ref[:, lanes] = _rms(o, gh_ref[:, lanes]).astype(BF16)

    for h in range(N_HEADS_SB):
        head_norm(osb_ref[h], h)
    for h in range(N_HEADS_DIL):
        l0, l1, l2 = l0_ref[h], l1_ref[h], l2_ref[h]
        m = jnp.maximum(jnp.maximum(l0, l1), l2)
        e0, e1, e2 = jnp.exp(l0 - m), jnp.exp(l1 - m), jnp.exp(l2 - m)
        o = (e0 * o0_ref[h] + e1 * o1_ref[h] + e2 * o2_ref[h]) / (e0 + e1 + e2)
        head_norm(o, N_HEADS_SB + h)
    y = jnp.dot(lhs_ref[...], w_ref[...], preferred_element_type=F32)
    out_ref[...] = x_ref[...] + _rms(y, gp_ref[...])


def _attn_out(x2, o_sb, dil, gain_head, w, gain_post, *, tm):
    n, d = x2.shape
    head_spec = pl.BlockSpec((N_HEADS_SB, tm, HEAD_DIM), lambda i: (0, i, 0))
    row_spec = pl.BlockSpec((tm, d), lambda i: (i, 0))
    vec_spec = pl.BlockSpec((1, d), lambda i: (0, 0))
    return pl.pallas_call(
        _attn_out_kernel,
        out_shape=jax.ShapeDtypeStruct((n, d), F32),
        grid=(n // tm,),
        in_specs=[row_spec] + [head_spec] * 7 + [
            vec_spec,
            pl.BlockSpec(w.shape, lambda i: (0, 0), pipeline_mode=pl.Buffered(1)),
            vec_spec,
        ],
        out_specs=row_spec,
        scratch_shapes=[pltpu.VMEM((tm, d), BF16)],
        compiler_params=_params(("parallel",)),
        name="attn_out",
    )(x2, o_sb, *dil, gain_head, w, gain_post)


def _ffn_kernel(x_ref, gpre_ref, wg_ref, wu_ref, wd_ref, gpost_ref, out_ref, h_ref, acc_ref):
    f = pl.program_id(1)

    @pl.when(f == 0)
    def _():
        h_ref[...] = _rms(x_ref[...], gpre_ref[...]).astype(BF16)
        acc_ref[...] = jnp.zeros_like(acc_ref)

    h = h_ref[...]
    g = jnp.dot(h, wg_ref[...], preferred_element_type=F32)
    u = jnp.dot(h, wu_ref[...], preferred_element_type=F32)
    act = (g * jax.nn.sigmoid(g) * u).astype(BF16)
    acc_ref[...] += jnp.dot(act, wd_ref[...], preferred_element_type=F32)

    @pl.when(f == pl.num_programs(1) - 1)
    def _():
        out_ref[...] = x_ref[...] + _rms(acc_ref[...], gpost_ref[...])


def _ffn(x2, gain_pre, w_gate_up, w_down, gain_post, *, tm, tf):
    n, d = x2.shape
    d_ff = w_down.shape[0]
    nf = d_ff // tf
    row_spec = pl.BlockSpec((tm, d), lambda i, f: (i, 0))
    vec_spec = pl.BlockSpec((1, d), lambda i, f: (0, 0))
    return pl.pallas_call(
        _ffn_kernel,
        out_shape=jax.ShapeDtypeStruct((n, d), F32),
        grid=(n // tm, nf),
        in_specs=[
            row_spec,
            vec_spec,
            pl.BlockSpec((d, tf), lambda i, f: (0, f)),
            pl.BlockSpec((d, tf), lambda i, f: (0, nf + f)),
            pl.BlockSpec((tf, d), lambda i, f: (f, 0)),
            vec_spec,
        ],
        out_specs=row_spec,
        scratch_shapes=[pltpu.VMEM((tm, d), BF16), pltpu.VMEM((tm, d), F32)],
        compiler_params=_params(("parallel", "arbitrary")),
        name="ffn",
    )(x2, gain_pre, w_gate_up, w_gate_up, w_down, gain_post)


def _pli_kernel(x_ref, p_ref, g_ref, wg_ref, wp_ref, out_ref):
    x = x_ref[...]
    h = _rms(x, g_ref[...]).astype(BF16)
    gate = jax.nn.sigmoid(jnp.dot(h, wg_ref[...], preferred_element_type=F32))
    emb = jnp.dot(p_ref[...].astype(BF16), wp_ref[...], preferred_element_type=F32)
    out_ref[...] = x + gate * emb


def _pli(x2, p2, gain, w_gate, w_proj, *, tm):
    n, d = x2.shape
    dp = p2.shape[1]
    row_spec = pl.BlockSpec((tm, d), lambda i: (i, 0))
    return pl.pallas_call(
        _pli_kernel,
        out_shape=jax.ShapeDtypeStruct((n, d), F32),
        grid=(n // tm,),
        in_specs=[
            row_spec,
            pl.BlockSpec((tm, dp), lambda i: (i, 0)),
            pl.BlockSpec((1, d), lambda i: (0, 0)),
            pl.BlockSpec(w_gate.shape, lambda i: (0, 0), pipeline_mode=pl.Buffered(1)),
            pl.BlockSpec(w_proj.shape, lambda i: (0, 0), pipeline_mode=pl.Buffered(1)),
        ],
        out_specs=row_spec,
        compiler_params=_params(("parallel",)),
        name="pli",
    )(x2, p2, gain, w_gate, w_proj)


def kernel(x, p, ln_mix_pre, w_in, ln_head, w_out, ln_mix_post, rel_bias, ln_ffn_pre, w_gate_up, w_down,
           ln_ffn_post, ln_pli, w_pli_gate, w_pli_proj):
    b, s, d = x.shape
    depth = w_in.shape[0]
    n = b * s
    assert d == N_HEADS * HEAD_DIM and w_in.shape[2] == 3 * d
    assert s % (BLOCK * DIL_GROUPS[-1][1]) == 0, "sequence must be a multiple of the widest dilated span"

    tables = _bias_tables(rel_bias)
    x2 = x.reshape(n, d)
    for i in range(depth):
        qkv = _qkv_proj(x2, ln_mix_pre[i][None], w_in[i].astype(BF16), tm=min(1024, n))
        qkv = qkv.reshape(qkv.shape[0], b, s, HEAD_DIM)
        o_sb = _sb_attention(qkv, b, s, tq=256).reshape(N_HEADS_SB, n, HEAD_DIM)
        dil = _dil_attention(qkv, tables, b, s)
        x2 = _attn_out(x2, o_sb, dil, ln_head[i][None], w_out[i].astype(BF16), ln_mix_post[i][None],
                       tm=min(256, n))
        x2 = _ffn(x2, ln_ffn_pre[i][None], w_gate_up[i].astype(BF16), w_down[i].astype(BF16),
                  ln_ffn_post[i][None], tm=min(512, n), tf=512)
        x2 = _pli(x2, p[i].reshape(n, -1), ln_pli[i][None], w_pli_gate[i].astype(BF16),
                  w_pli_proj[i].astype(BF16), tm=min(512, n))
    return x2.reshape(b, s, d)
```

```python
import functools
import math

import jax
import jax.numpy as jnp
import numpy as np
from jax import lax
from jax.experimental import pallas as pl
from jax.experimental.pallas import tpu as pltpu

HEAD_DIM = 128
N_HEADS_SB = 8
N_HEADS_DIL = 8
N_HEADS = N_HEADS_SB + N_HEADS_DIL
BLOCK = 128
DIL_GROUPS = ((128, 1), (512, 4), (2048, 16))
NUM_BUCKETS = 32
MAX_DISTANCE = 2048
RMS_EPS = 1e-6
MASK_VALUE = -1e30

VMEM_LIMIT_BYTES = 56 * 1024 * 1024

F32 = jnp.float32
BF16 = jnp.bfloat16
LOG2E = math.log2(math.e)
_NT = (((1,), (1,)), ((), ()))


def _params(semantics):
    return pltpu.CompilerParams(dimension_semantics=semantics, vmem_limit_bytes=VMEM_LIMIT_BYTES)


def _rms(x, gain):
    ms = jnp.mean(x * x, axis=-1, keepdims=True)
    return x * lax.rsqrt(ms + RMS_EPS) * gain


def _qkv_kernel(x_ref, g_ref, w_ref, sb_ref, dl_ref, h_ref, *, n_sb_steps):
    j = pl.program_id(1)

    @pl.when(j == 0)
    def _():
        h_ref[...] = _rms(x_ref[...], g_ref[...]).astype(BF16)

    r = jnp.dot(h_ref[...], w_ref[...], preferred_element_type=F32)

    @pl.when(j < n_sb_steps)
    def _():
        for hh in range(sb_ref.shape[0]):
            sb_ref[hh] = r[:, hh * HEAD_DIM:(hh + 1) * HEAD_DIM].astype(BF16)

    @pl.when(j >= n_sb_steps)
    def _():
        for hh in range(dl_ref.shape[0]):
            dl_ref[hh] = r[:, hh * HEAD_DIM:(hh + 1) * HEAD_DIM]


def _qkv_proj(x2, gain, w, *, tm):
    n, d = x2.shape
    heads_per_step = 8
    tn = heads_per_step * HEAD_DIM
    n_sb_steps = 3 * N_HEADS_SB // heads_per_step
    n_dl_steps = 3 * N_HEADS_DIL // heads_per_step
    assert w.shape[1] == (n_sb_steps + n_dl_steps) * tn
    return pl.pallas_call(
        functools.partial(_qkv_kernel, n_sb_steps=n_sb_steps),
        out_shape=[
            jax.ShapeDtypeStruct((3 * N_HEADS_SB, n, HEAD_DIM), BF16),
            jax.ShapeDtypeStruct((3 * N_HEADS_DIL, n, HEAD_DIM), F32),
        ],
        grid=(n // tm, n_sb_steps + n_dl_steps),
        in_specs=[
            pl.BlockSpec((tm, d), lambda i, j: (i, 0)),
            pl.BlockSpec((1, d), lambda i, j: (0, 0)),
            pl.BlockSpec((d, tn), lambda i, j: (0, j)),
        ],
        out_specs=[
            pl.BlockSpec((heads_per_step, tm, HEAD_DIM),
                         lambda i, j: (jnp.minimum(j, n_sb_steps - 1), i, 0)),
            pl.BlockSpec((heads_per_step, tm, HEAD_DIM),
                         lambda i, j: (jnp.maximum(j - n_sb_steps, 0), i, 0)),
        ],
        scratch_shapes=[pltpu.VMEM((tm, d), BF16)],
        compiler_params=_params(("parallel", "arbitrary")),
        name="qkv_proj",
    )(x2, gain, w)


def _sb_kernel(q_ref, k_ref, v_ref, g_ref, o_ref, u_ref, c_ref, acc_ref, *, tq, tk, scale):
    i = pl.program_id(2)
    nsub = tq // tk
    row_u = lax.broadcasted_iota(jnp.int32, (tk, tk), 0)
    col_u = lax.broadcasted_iota(jnp.int32, (tk, tk), 1)
    u_ref[...] = jnp.where(row_u >= col_u, 1.0, 0.0).astype(BF16)

    def tile(r0, kstart, masked):
        rows = tq - r0
        q = q_ref[r0:, :]
        k = k_ref[pl.ds(kstart, tk), :]
        v = v_ref[pl.ds(kstart, tk), :]
        z2 = lax.dot_general(q, k, _NT, preferred_element_type=F32) * (scale * LOG2E)
        neg_abs = pltpu.bitcast(pltpu.bitcast(z2, jnp.uint32) | jnp.uint32(0x80000000), F32)
        sp2 = jnp.maximum(z2, 0.0) + jnp.log2(1.0 + jnp.exp2(neg_abs))
        if masked:
            earlier = (lax.broadcasted_iota(jnp.int32, (rows, tk), 1)
                       < lax.broadcasted_iota(jnp.int32, (rows, tk), 0))
            sp2 = jnp.where(earlier, sp2, 0.0)
        hi = sp2.astype(BF16)
        lo = (sp2 - hi.astype(F32)).astype(BF16)
        u = u_ref[...]
        incl = (jnp.dot(hi, u, preferred_element_type=F32)
                + jnp.dot(lo, u, preferred_element_type=F32))
        c = c_ref[r0:, :]
        a = jnp.exp2(z2 - incl - c)
        if masked:
            a = jnp.where(earlier, a, 0.0)
        acc_ref[r0:, :] += jnp.dot(a.astype(BF16), v, preferred_element_type=F32)
        c_ref[r0:, :] = c + incl[:, 0:1]

    c_ref[...] = jnp.zeros_like(c_ref)
    acc_ref[...] = jnp.zeros_like(acc_ref)
    for jj in reversed(range(nsub)):
        tile(jj * tk, pl.multiple_of(i * tq + jj * tk, tk), True)

    def body(s, carry):
        tile(0, pl.multiple_of((i * nsub - 1 - s) * tk, tk), False)
        return carry

    lax.fori_loop(0, i * nsub, body, 0)
    o_ref[...] = _rms(acc_ref[...], g_ref[...]).astype(BF16)


def _sb_attention(qkv, gain_heads, b, s, *, tq, tk):
    kern = functools.partial(_sb_kernel, tq=tq, tk=tk, scale=1.0 / math.sqrt(HEAD_DIM))
    return pl.pallas_call(
        kern,
        out_shape=jax.ShapeDtypeStruct((b, s, N_HEADS_SB * HEAD_DIM), BF16),
        grid=(b, N_HEADS_SB, s // tq),
        in_specs=[
            pl.BlockSpec((None, None, tq, HEAD_DIM), lambda bi, h, i: (h, bi, i, 0)),
            pl.BlockSpec((None, None, s, HEAD_DIM), lambda bi, h, i: (N_HEADS_SB + h, bi, 0, 0)),
            pl.BlockSpec((None, None, s, HEAD_DIM), lambda bi, h, i: (2 * N_HEADS_SB + h, bi, 0, 0)),
            pl.BlockSpec((None, 1, HEAD_DIM), lambda bi, h, i: (h, 0, 0)),
        ],
        out_specs=pl.BlockSpec((None, tq, HEAD_DIM), lambda bi, h, i: (bi, i, h)),
        scratch_shapes=[
            pltpu.VMEM((tk, tk), BF16),
            pltpu.VMEM((tq, 1), F32),
            pltpu.VMEM((tq, HEAD_DIM), F32),
        ],
        compiler_params=_params(("parallel", "parallel", "arbitrary")),
        name="sb_attn",
    )(qkv, qkv, qkv, gain_heads)


def _t5_bucket_np(dist):
    max_exact = NUM_BUCKETS // 2
    d = np.maximum(dist, 1).astype(np.float64)
    large = max_exact + (np.log(d / max_exact) / math.log(MAX_DISTANCE / max_exact)
                         * (NUM_BUCKETS - max_exact)).astype(np.int64)
    large = np.minimum(large, NUM_BUCKETS - 1)
    return np.where(dist < max_exact, dist, large)


def _bias_tables(rel_bias):
    qi = np.arange(BLOCK)[:, None]
    ki = np.arange(2 * BLOCK)[None, :]
    rel = BLOCK + qi - ki
    band = (rel >= 0) & (rel <= BLOCK)
    tables = []
    for _, dilation in DIL_GROUPS:
        bucket = _t5_bucket_np(np.maximum(rel, 0) * dilation)
        bias = rel_bias.astype(F32)[bucket]
        bias = jnp.where(band[:, :, None], bias, MASK_VALUE)
        tables.append(bias.transpose(2, 0, 1))
    return jnp.stack(tables, axis=0)


def _dil_kernel(tbl_ref, q_ref, k_ref, v_ref, g_ref, o_ref, og_ref, lg_ref, *, seq, scale, blocks_per_step):
    def rows(start, count, d):
        if d == 1:
            return pl.ds(start if isinstance(start, int) else pl.multiple_of(start, BLOCK), count)
        return pl.ds(start, count, stride=d)

    def load(ref, start, count, d):
        return ref[rows(start, count, d), :].astype(BF16)

    def tile(g, d, q, k2, v2, tbl, start):
        s = lax.dot_general(q, k2, _NT, preferred_element_type=F32) * scale + tbl
        m = jnp.max(s, axis=-1, keepdims=True)
        p = jnp.exp(s - m)
        denom = jnp.sum(p, axis=-1, keepdims=True)
        o = jnp.dot(p.astype(BF16), v2, preferred_element_type=F32) / denom
        og_ref[g, rows(start, BLOCK, d), :] = o
        lg_ref[g, rows(start, BLOCK, d), :] = jnp.broadcast_to(m + jnp.log(denom), (BLOCK, HEAD_DIM))

    for g, (_, d) in enumerate(DIL_GROUPS):
        nb = seq // (BLOCK * d)
        nblk = min(blocks_per_step, nb)
        span = BLOCK * d

        def first(r, g=g, d=d, nblk=nblk, span=span):
            q = load(q_ref, r, nblk * BLOCK, d)
            k = load(k_ref, r, nblk * BLOCK, d)
            v = load(v_ref, r, nblk * BLOCK, d)
            tile(g, d, q[:BLOCK], k[:BLOCK], v[:BLOCK], tbl_ref[g, :, BLOCK:], r)
            for m in range(1, nblk):
                tile(g, d, q[m * BLOCK:(m + 1) * BLOCK], k[(m - 1) * BLOCK:(m + 1) * BLOCK],
                     v[(m - 1) * BLOCK:(m + 1) * BLOCK], tbl_ref[g], r + m * span)

        def later(r, n0, g=g, d=d, nblk=nblk, span=span):
            q = load(q_ref, r + n0 * span, nblk * BLOCK, d)
            k = load(k_ref, r + (n0 - 1) * span, (nblk + 1) * BLOCK, d)
            v = load(v_ref, r + (n0 - 1) * span, (nblk + 1) * BLOCK, d)
            for m in range(nblk):
                tile(g, d, q[m * BLOCK:(m + 1) * BLOCK], k[m * BLOCK:(m + 2) * BLOCK],
                     v[m * BLOCK:(m + 2) * BLOCK], tbl_ref[g], r + (n0 + m) * span)

        def subsequence(r, nb=nb, nblk=nblk, first=first, later=later):
            first(r)
            if nb > nblk:
                def body(t, carry):
                    later(r, t * nblk)
                    return carry
                lax.fori_loop(1, nb // nblk, body, 0)

        if d == 1:
            subsequence(0)
        else:
            unroll = max(1, min(d, 16 // nb))

            def r_body(t, carry, subsequence=subsequence, unroll=unroll):
                for j in range(unroll):
                    subsequence(t * unroll + j)
                return carry
            lax.fori_loop(0, d // unroll, r_body, 0)

    chunk = min(1024, seq)
    gain = g_ref[...]

    def finish(t, carry):
        rws = pl.ds(pl.multiple_of(t * chunk, chunk), chunk)
        l0, l1, l2 = lg_ref[0, rws, :], lg_ref[1, rws, :], lg_ref[2, rws, :]
        m = jnp.maximum(jnp.maximum(l0, l1), l2)
        e0, e1, e2 = jnp.exp(l0 - m), jnp.exp(l1 - m), jnp.exp(l2 - m)
        o = (e0 * og_ref[0, rws, :] + e1 * og_ref[1, rws, :] + e2 * og_ref[2, rws, :]) / (e0 + e1 + e2)
        o_ref[rws, :] = _rms(o, gain).astype(BF16)
        return carry

    lax.fori_loop(0, seq // chunk, finish, 0)


def _dil_attention(qkv, tables, gain_heads, b, s):
    n_groups = len(DIL_GROUPS)
    kern = functools.partial(_dil_kernel, seq=s, scale=1.0 / math.sqrt(HEAD_DIM), blocks_per_step=8)
    return pl.pallas_call(
        kern,
        out_shape=jax.ShapeDtypeStruct((b, s, N_HEADS_DIL * HEAD_DIM), BF16),
        grid=(b, N_HEADS_DIL),
        in_specs=[
            pl.BlockSpec((n_groups, None, BLOCK, 2 * BLOCK), lambda bi, h: (0, h, 0, 0)),
            pl.BlockSpec((None, None, s, HEAD_DIM), lambda bi, h: (h, bi, 0, 0)),
            pl.BlockSpec((None, None, s, HEAD_DIM), lambda bi, h: (N_HEADS_DIL + h, bi, 0, 0)),
            pl.BlockSpec((None, None, s, HEAD_DIM), lambda bi, h: (2 * N_HEADS_DIL + h, bi, 0, 0)),
            pl.BlockSpec((None, 1, HEAD_DIM), lambda bi, h: (N_HEADS_SB + h, 0, 0)),
        ],
        out_specs=pl.BlockSpec((None, s, HEAD_DIM), lambda bi, h: (bi, 0, h)),
        scratch_shapes=[
            pltpu.VMEM((n_groups, s, HEAD_DIM), F32),
            pltpu.VMEM((n_groups, s, HEAD_DIM), F32),
        ],
        compiler_params=_params(("parallel", "parallel")),
        name="dil_attn",
    )(tables, qkv, qkv, qkv, gain_heads)


def _attn_out_kernel(x_ref, osb_ref, odl_ref, w_ref, gp_ref, out_ref):
    k_sb = osb_ref.shape[1]
    y = (jnp.dot(osb_ref[...], w_ref[:k_sb, :], preferred_element_type=F32)
         + jnp.dot(odl_ref[...], w_ref[k_sb:, :], preferred_element_type=F32))
    out_ref[...] = x_ref[...] + _rms(y, gp_ref[...])


def _attn_out(x2, o_sb, o_dl, w, gain_post, *, tm):
    n, d = x2.shape
    row_spec = pl.BlockSpec((tm, d), lambda i: (i, 0))
    return pl.pallas_call(
        _attn_out_kernel,
        out_shape=jax.ShapeDtypeStruct((n, d), F32),
        grid=(n // tm,),
        in_specs=[
            row_spec,
            pl.BlockSpec((tm, o_sb.shape[1]), lambda i: (i, 0)),
            pl.BlockSpec((tm, o_dl.shape[1]), lambda i: (i, 0)),
            pl.BlockSpec(w.shape, lambda i: (0, 0), pipeline_mode=pl.Buffered(1)),
            pl.BlockSpec((1, d), lambda i: (0, 0)),
        ],
        out_specs=row_spec,
        compiler_params=_params(("parallel",)),
        name="attn_out",
    )(x2, o_sb, o_dl, w, gain_post)


def _ffn_kernel(x_ref, gpre_ref, wg_ref, wu_ref, wd_ref, gpost_ref, out_ref, h_ref, acc_ref):
    f = pl.program_id(1)

    @pl.when(f == 0)
    def _():
        h_ref[...] = _rms(x_ref[...], gpre_ref[...]).astype(BF16)
        acc_ref[...] = jnp.zeros_like(acc_ref)

    h = h_ref[...]
    g = jnp.dot(h, wg_ref[...], preferred_element_type=F32)
    u = jnp.dot(h, wu_ref[...], preferred_element_type=F32)
    act = (g * jax.nn.sigmoid(g) * u).astype(BF16)
    acc_ref[...] += jnp.dot(act, wd_ref[...], preferred_element_type=F32)

    @pl.when(f == pl.num_programs(1) - 1)
    def _():
        out_ref[...] = x_ref[...] + _rms(acc_ref[...], gpost_ref[...])


def _ffn(x2, gain_pre, w_gate_up, w_down, gain_post, *, tm, tf):
    n, d = x2.shape
    d_ff = w_down.shape[0]
    nf = d_ff // tf
    row_spec = pl.BlockSpec((tm, d), lambda i, f: (i, 0))
    vec_spec = pl.BlockSpec((1, d), lambda i, f: (0, 0))
    return pl.pallas_call(
        _ffn_kernel,
        out_shape=jax.ShapeDtypeStruct((n, d), F32),
        grid=(n // tm, nf),
        in_specs=[
            row_spec,
            vec_spec,
            pl.BlockSpec((d, tf), lambda i, f: (0, f)),
            pl.BlockSpec((d, tf), lambda i, f: (0, nf + f)),
            pl.BlockSpec((tf, d), lambda i, f: (f, 0)),
            vec_spec,
        ],
        out_specs=row_spec,
        scratch_shapes=[pltpu.VMEM((tm, d), BF16), pltpu.VMEM((tm, d), F32)],
        compiler_params=_params(("parallel", "arbitrary")),
        name="ffn",
    )(x2, gain_pre, w_gate_up, w_gate_up, w_down, gain_post)


def _pli_kernel(x_ref, p_ref, g_ref, wg_ref, wp_ref, out_ref):
    x = x_ref[...]
    h = _rms(x, g_ref[...]).astype(BF16)
    gate = jax.nn.sigmoid(jnp.dot(h, wg_ref[...], preferred_element_type=F32))
    emb = jnp.dot(p_ref[...].astype(BF16), wp_ref[...], preferred_element_type=F32)
    out_ref[...] = x + gate * emb


def _pli(x2, p2, gain, w_gate, w_proj, *, tm):
    n, d = x2.shape
    dp = p2.shape[1]
    row_spec = pl.BlockSpec((tm, d), lambda i: (i, 0))
    return pl.pallas_call(
        _pli_kernel,
        out_shape=jax.ShapeDtypeStruct((n, d), F32),
        grid=(n // tm,),
        in_specs=[
            row_spec,
            pl.BlockSpec((tm, dp), lambda i: (i, 0)),
            pl.BlockSpec((1, d), lambda i: (0, 0)),
            pl.BlockSpec(w_gate.shape, lambda i: (0, 0), pipeline_mode=pl.Buffered(1)),
            pl.BlockSpec(w_proj.shape, lambda i: (0, 0), pipeline_mode=pl.Buffered(1)),
        ],
        out_specs=row_spec,
        compiler_params=_params(("parallel",)),
        name="pli",
    )(x2, p2, gain, w_gate, w_proj)


def kernel(x, p, ln_mix_pre, w_in, ln_head, w_out, ln_mix_post, rel_bias, ln_ffn_pre, w_gate_up, w_down,
           ln_ffn_post, ln_pli, w_pli_gate, w_pli_proj):
    b, s, d = x.shape
    depth = w_in.shape[0]
    n = b * s
    assert d == N_HEADS * HEAD_DIM and w_in.shape[2] == 3 * d
    assert s % (BLOCK * DIL_GROUPS[-1][1]) == 0, "sequence must be a multiple of the widest dilated span"
    tq = min(1024, s)

    tables = _bias_tables(rel_bias)
    x2 = x.reshape(n, d)
    for i in range(depth):
        gain_heads = ln_head[i].reshape(N_HEADS, 1, HEAD_DIM)
        qkv_sb, qkv_dl = _qkv_proj(x2, ln_mix_pre[i][None], w_in[i].astype(BF16), tm=min(1024, n))
        o_sb = _sb_attention(qkv_sb.reshape(-1, b, s, HEAD_DIM), gain_heads, b, s, tq=tq, tk=256)
        o_dl = _dil_attention(qkv_dl.reshape(-1, b, s, HEAD_DIM), tables, gain_heads, b, s)
        x2 = _attn_out(x2, o_sb.reshape(n, -1), o_dl.reshape(n, -1), w_out[i].astype(BF16),
                       ln_mix_post[i][None], tm=min(512, n))
        x2 = _ffn(x2, ln_ffn_pre[i][None], w_gate_up[i].astype(BF16), w_down[i].astype(BF16),
                  ln_ffn_post[i][None], tm=min(512, n), tf=512)
        x2 = _pli(x2, p[i].reshape(n, -1), ln_pli[i][None], w_pli_gate[i].astype(BF16),
                  w_pli_proj[i].astype(BF16), tm=min(512, n))
    return x2.reshape(b, s, d)
```

```python
import functools
import math

import jax
import jax.numpy as jnp
import numpy as np
from jax import lax
from jax.experimental import pallas as pl
from jax.experimental.pallas import tpu as pltpu

HEAD_DIM = 128
N_HEADS_SB = 8
N_HEADS_DIL = 8
N_HEADS = N_HEADS_SB + N_HEADS_DIL
BLOCK = 128
DIL_GROUPS = ((128, 1), (512, 4), (2048, 16))
INTERLEAVE = 4
NUM_BUCKETS = 32
MAX_DISTANCE = 2048
RMS_EPS = 1e-6
MASK_VALUE = -1e30

VMEM_LIMIT_BYTES = 56 * 1024 * 1024

F32 = jnp.float32
BF16 = jnp.bfloat16
LOG2E = math.log2(math.e)
_NT = (((1,), (1,)), ((), ()))


def _params(semantics):
    return pltpu.CompilerParams(dimension_semantics=semantics, vmem_limit_bytes=VMEM_LIMIT_BYTES)


def _rms(x, gain):
    ms = jnp.mean(x * x, axis=-1, keepdims=True)
    return x * lax.rsqrt(ms + RMS_EPS) * gain


def _qkv_kernel(x_ref, g_ref, w_ref, sb_ref, dl_ref, h_ref, *, n_sb_steps):
    j = pl.program_id(1)

    @pl.when(j == 0)
    def _():
        h_ref[...] = _rms(x_ref[...], g_ref[...]).astype(BF16)

    r = jnp.dot(h_ref[...], w_ref[...], preferred_element_type=F32)

    @pl.when(j < n_sb_steps)
    def _():
        for hh in range(sb_ref.shape[0]):
            sb_ref[hh] = r[:, hh * HEAD_DIM:(hh + 1) * HEAD_DIM].astype(BF16)

    @pl.when(j >= n_sb_steps)
    def _():
        for hh in range(dl_ref.shape[0]):
            dl_ref[hh] = r[:, hh * HEAD_DIM:(hh + 1) * HEAD_DIM]


def _qkv_proj(x2, gain, w, *, tm):
    n, d = x2.shape
    heads_per_step = 8
    tn = heads_per_step * HEAD_DIM
    n_sb_steps = 3 * N_HEADS_SB // heads_per_step
    n_dl_steps = 3 * N_HEADS_DIL // heads_per_step
    assert w.shape[1] == (n_sb_steps + n_dl_steps) * tn
    return pl.pallas_call(
        functools.partial(_qkv_kernel, n_sb_steps=n_sb_steps),
        out_shape=[
            jax.ShapeDtypeStruct((3 * N_HEADS_SB, n, HEAD_DIM), BF16),
            jax.ShapeDtypeStruct((3 * N_HEADS_DIL, n, HEAD_DIM), F32),
        ],
        grid=(n // tm, n_sb_steps + n_dl_steps),
        in_specs=[
            pl.BlockSpec((tm, d), lambda i, j: (i, 0)),
            pl.BlockSpec((1, d), lambda i, j: (0, 0)),
            pl.BlockSpec((d, tn), lambda i, j: (0, j)),
        ],
        out_specs=[
            pl.BlockSpec((heads_per_step, tm, HEAD_DIM),
                         lambda i, j: (jnp.minimum(j, n_sb_steps - 1), i, 0)),
            pl.BlockSpec((heads_per_step, tm, HEAD_DIM),
                         lambda i, j: (jnp.maximum(j - n_sb_steps, 0), i, 0)),
        ],
        scratch_shapes=[pltpu.VMEM((tm, d), BF16)],
        compiler_params=_params(("parallel", "arbitrary")),
        name="qkv_proj",
    )(x2, gain, w)


def _sb_kernel(q_ref, k_ref, v_ref, g_ref, o_ref, u_ref, c_ref, acc_ref, z_ref, hi_ref, lo_ref, *, tq, tk, scale):
    i = pl.program_id(2)
    nsub = tq // tk
    row_u = lax.broadcasted_iota(jnp.int32, (tk, tk), 0)
    col_u = lax.broadcasted_iota(jnp.int32, (tk, tk), 1)
    u_ref[...] = jnp.where(row_u >= col_u, 1.0, 0.0).astype(BF16)
    earlier = col_u < row_u

    def logits(r0, kstart):
        k = k_ref[pl.ds(kstart, tk), :]
        z2 = lax.dot_general(q_ref[r0:, :], k, _NT, preferred_element_type=F32) * (scale * LOG2E)
        neg_abs = pltpu.bitcast(pltpu.bitcast(z2, jnp.uint32) | jnp.uint32(0x80000000), F32)
        sp2 = jnp.maximum(z2, 0.0) + jnp.log2(1.0 + jnp.exp2(neg_abs))
        return z2, sp2

    def split(sp2):
        hi = sp2.astype(BF16)
        return hi, (sp2 - hi.astype(F32)).astype(BF16)

    def mask_top(x):
        top = jnp.where(earlier, x[:tk], 0.0)
        return top if x.shape[0] == tk else jnp.concatenate([top, x[tk:]], axis=0)

    def accumulate(r0, kstart, z2, hi, lo, diagonal):
        u = u_ref[...]
        incl = (jnp.dot(hi, u, preferred_element_type=F32)
                + jnp.dot(lo, u, preferred_element_type=F32))
        c = c_ref[r0:, :]
        a = jnp.exp2(z2 - incl - c)
        if diagonal:
            a = mask_top(a)
        v = v_ref[pl.ds(kstart, tk), :]
        acc_ref[r0:, :] += jnp.dot(a.astype(BF16), v, preferred_element_type=F32)
        c_ref[r0:, :] = c + incl[:, 0:1]

    def stage_a(kstart, slot):
        z2, sp2 = logits(0, kstart)
        hi, lo = split(sp2)
        z_ref[slot] = z2
        hi_ref[slot] = hi
        lo_ref[slot] = lo

    def stage_b(kstart, slot):
        accumulate(0, kstart, z_ref[slot], hi_ref[slot], lo_ref[slot], False)

    c_ref[...] = jnp.zeros_like(c_ref)
    acc_ref[...] = jnp.zeros_like(acc_ref)

    diag = []
    for jj in reversed(range(nsub)):
        kstart = pl.multiple_of(i * tq + jj * tk, tk)
        z2, sp2 = logits(jj * tk, kstart)
        diag.append((jj * tk, kstart, z2) + split(mask_top(sp2)))
    for r0, kstart, z2, hi, lo in diag:
        accumulate(r0, kstart, z2, hi, lo, True)

    n = i * nsub

    def kstart_of(s):
        return pl.multiple_of((n - 1 - s) * tk, tk)

    @pl.when(n > 0)
    def _():
        stage_a(kstart_of(0), 0)

        def body(t, carry):
            stage_a(kstart_of(2 * t + 1), 1)
            stage_b(kstart_of(2 * t), 0)
            stage_a(kstart_of(2 * t + 2), 0)
            stage_b(kstart_of(2 * t + 1), 1)
            return carry

        lax.fori_loop(0, n // 2 - 1, body, 0)
        stage_a(kstart_of(n - 1), 1)
        stage_b(kstart_of(n - 2), 0)
        stage_b(kstart_of(n - 1), 1)

    o_ref[...] = _rms(acc_ref[...], g_ref[...]).astype(BF16)


def _sb_attention(qkv, gain_heads, b, s, *, tq, tk):
    kern = functools.partial(_sb_kernel, tq=tq, tk=tk, scale=1.0 / math.sqrt(HEAD_DIM))
    return pl.pallas_call(
        kern,
        out_shape=jax.ShapeDtypeStruct((b, s, N_HEADS_SB * HEAD_DIM), BF16),
        grid=(b, N_HEADS_SB, s // tq),
        in_specs=[
            pl.BlockSpec((None, None, tq, HEAD_DIM), lambda bi, h, i: (h, bi, i, 0)),
            pl.BlockSpec((None, None, s, HEAD_DIM), lambda bi, h, i: (N_HEADS_SB + h, bi, 0, 0)),
            pl.BlockSpec((None, None, s, HEAD_DIM), lambda bi, h, i: (2 * N_HEADS_SB + h, bi, 0, 0)),
            pl.BlockSpec((None, 1, HEAD_DIM), lambda bi, h, i: (h, 0, 0)),
        ],
        out_specs=pl.BlockSpec((None, tq, HEAD_DIM), lambda bi, h, i: (bi, i, h)),
        scratch_shapes=[
            pltpu.VMEM((tk, tk), BF16),
            pltpu.VMEM((tq, 1), F32),
            pltpu.VMEM((tq, HEAD_DIM), F32),
            pltpu.VMEM((2, tq, tk), F32),
            pltpu.VMEM((2, tq, tk), BF16),
            pltpu.VMEM((2, tq, tk), BF16),
        ],
        compiler_params=_params(("parallel", "parallel", "arbitrary")),
        name="sb_attn",
    )(qkv, qkv, qkv, gain_heads)


def _t5_bucket_np(dist):
    max_exact = NUM_BUCKETS // 2
    d = np.maximum(dist, 1).astype(np.float64)
    large = max_exact + (np.log(d / max_exact) / math.log(MAX_DISTANCE / max_exact)
                         * (NUM_BUCKETS - max_exact)).astype(np.int64)
    large = np.minimum(large, NUM_BUCKETS - 1)
    return np.where(dist < max_exact, dist, large)


def _bias_tables(rel_bias):
    qi = np.arange(BLOCK)[:, None]
    ki = np.arange(2 * BLOCK)[None, :]
    rel = BLOCK + qi - ki
    band = (rel >= 0) & (rel <= BLOCK)
    bias_by_bucket = rel_bias.astype(F32)
    tables = []
    for _, dilation in DIL_GROUPS:
        bucket = _t5_bucket_np(np.maximum(rel, 0) * dilation)
        table = jnp.full((bias_by_bucket.shape[1], BLOCK, 2 * BLOCK), MASK_VALUE, F32)
        for bkt in np.unique(bucket[band]):
            table = jnp.where((band & (bucket == bkt))[None], bias_by_bucket[bkt][:, None, None], table)
        tables.append(table)
    return jnp.stack(tables, axis=0)


def _dil_kernel(tbl_ref, q_ref, k_ref, v_ref, g_ref, o_ref, t_ref, og_ref, lg_ref, *, seq, scale, tiles_per_step):
    quarter = seq // INTERLEAVE
    for x, ref in enumerate((q_ref, k_ref, v_ref)):
        for a in range(INTERLEAVE):
            t_ref[x, a * quarter:(a + 1) * quarter, :] = ref[pl.ds(a, quarter, stride=INTERLEAVE), :]

    def rows(start, count, stride):
        if stride == 1:
            return pl.ds(start if isinstance(start, int) else pl.multiple_of(start, BLOCK), count)
        return pl.ds(start, count, stride=stride)

    sources = (
        (lambda x, rws: (q_ref, k_ref, v_ref)[x][rws, :], 1, 1, lambda r: 0),
        (lambda x, rws: t_ref[x, rws, :], 1, INTERLEAVE, lambda r: r * quarter),
        (lambda x, rws: t_ref[x, rws, :], INTERLEAVE, INTERLEAVE * INTERLEAVE,
         lambda r: (r % INTERLEAVE) * quarter + r // INTERLEAVE),
    )

    def emit(g, stride, tiles):
        logits = [lax.dot_general(q, k2, _NT, preferred_element_type=F32) * scale + tbl
                  for q, k2, _, tbl, _ in tiles]
        stats = []
        for s in logits:
            m = jnp.max(s, axis=-1, keepdims=True)
            p = jnp.exp(s - m)
            stats.append((m, p.astype(BF16), jnp.sum(p, axis=-1, keepdims=True)))
        for (m, p, denom), (_, _, v2, _, start) in zip(stats, tiles):
            o = jnp.dot(p, v2, preferred_element_type=F32) / denom
            og_ref[g, rows(start, BLOCK, stride), :] = o
            lg_ref[g, rows(start, BLOCK, stride), :] = jnp.broadcast_to(m + jnp.log(denom), (BLOCK, HEAD_DIM))

    for g, (_, d) in enumerate(DIL_GROUPS):
        src, stride, n_sub, base_of = sources[g]
        assert n_sub == d
        nb = seq // (BLOCK * d)
        nblk = min(tiles_per_step, nb)
        span = BLOCK * stride

        def load(x, start, count, src=src, stride=stride):
            return src(x, rows(start, count, stride)).astype(BF16)

        def first(base, g=g, nblk=nblk, span=span, load=load):
            q = load(0, base, nblk * BLOCK)
            k = load(1, base, nblk * BLOCK)
            v = load(2, base, nblk * BLOCK)
            tiles = [(q[:BLOCK], k[:BLOCK], v[:BLOCK], tbl_ref[g, :, BLOCK:], base)]
            for m in range(1, nblk):
                tiles.append((q[m * BLOCK:(m + 1) * BLOCK], k[(m - 1) * BLOCK:(m + 1) * BLOCK],
                              v[(m - 1) * BLOCK:(m + 1) * BLOCK], tbl_ref[g], base + m * span))
            return tiles

        def later(base, n0, g=g, nblk=nblk, span=span, load=load):
            q = load(0, base + n0 * span, nblk * BLOCK)
            k = load(1, base + (n0 - 1) * span, (nblk + 1) * BLOCK)
            v = load(2, base + (n0 - 1) * span, (nblk + 1) * BLOCK)
            return [(q[m * BLOCK:(m + 1) * BLOCK], k[m * BLOCK:(m + 2) * BLOCK],
                     v[m * BLOCK:(m + 2) * BLOCK], tbl_ref[g], base + (n0 + m) * span) for m in range(nblk)]

        unroll = max(1, min(d, tiles_per_step // nb))

        def r_body(t, carry, g=g, stride=stride, nb=nb, nblk=nblk, unroll=unroll, first=first, later=later,
                   base_of=base_of):
            bases = [base_of(t * unroll + j) for j in range(unroll)]
            emit(g, stride, [tl for base in bases for tl in first(base)])
            if nb > nblk:
                def body(u, carry2):
                    emit(g, stride, [tl for base in bases for tl in later(base, u * nblk)])
                    return carry2
                lax.fori_loop(1, nb // nblk, body, 0)
            return carry

        if d == unroll:
            r_body(0, 0)
        else:
            lax.fori_loop(0, d // unroll, r_body, 0)

    gain = g_ref[...]

    def finish(a, carry):
        seq_rows = pl.ds(a, quarter, stride=INTERLEAVE)
        t_rows = pl.ds(pl.multiple_of(a * quarter, BLOCK), quarter)
        l0, l1, l2 = lg_ref[0, seq_rows, :], lg_ref[1, t_rows, :], lg_ref[2, t_rows, :]
        m = jnp.maximum(jnp.maximum(l0, l1), l2)
        e0, e1, e2 = jnp.exp(l0 - m), jnp.exp(l1 - m), jnp.exp(l2 - m)
        o = ((e0 * og_ref[0, seq_rows, :] + e1 * og_ref[1, t_rows, :] + e2 * og_ref[2, t_rows, :])
             / (e0 + e1 + e2))
        o_ref[seq_rows, :] = _rms(o, gain)
        return carry

    lax.fori_loop(0, INTERLEAVE, finish, 0)


def _dil_attention(qkv, tables, gain_heads, b, s):
    n_groups = len(DIL_GROUPS)
    assert tuple(d for _, d in DIL_GROUPS) == (1, INTERLEAVE, INTERLEAVE * INTERLEAVE)
    kern = functools.partial(_dil_kernel, seq=s, scale=1.0 / math.sqrt(HEAD_DIM), tiles_per_step=16)
    return pl.pallas_call(
        kern,
        out_shape=jax.ShapeDtypeStruct((b, s, N_HEADS_DIL * HEAD_DIM), F32),
        grid=(b, N_HEADS_DIL),
        in_specs=[
            pl.BlockSpec((n_groups, None, BLOCK, 2 * BLOCK), lambda bi, h: (0, h, 0, 0)),
            pl.BlockSpec((None, None, s, HEAD_DIM), lambda bi, h: (h, bi, 0, 0)),
            pl.BlockSpec((None, None, s, HEAD_DIM), lambda bi, h: (N_HEADS_DIL + h, bi, 0, 0)),
            pl.BlockSpec((None, None, s, HEAD_DIM), lambda bi, h: (2 * N_HEADS_DIL + h, bi, 0, 0)),
            pl.BlockSpec((None, 1, HEAD_DIM), lambda bi, h: (N_HEADS_SB + h, 0, 0)),
        ],
        out_specs=pl.BlockSpec((None, s, HEAD_DIM), lambda bi, h: (bi, 0, h)),
        scratch_shapes=[
            pltpu.VMEM((3, s, HEAD_DIM), F32),
            pltpu.VMEM((n_groups, s, HEAD_DIM), F32),
            pltpu.VMEM((n_groups, s, HEAD_DIM), F32),
        ],
        compiler_params=_params(("parallel", "parallel")),
        name="dil_attn",
    )(tables, qkv, qkv, qkv, gain_heads)


def _attn_out_kernel(x_ref, osb_ref, odl_ref, w_ref, gp_ref, out_ref):
    k_sb = osb_ref.shape[1]
    y = (jnp.dot(osb_ref[...], w_ref[:k_sb, :], preferred_element_type=F32)
         + jnp.dot(odl_ref[...].astype(BF16), w_ref[k_sb:, :], preferred_element_type=F32))
    out_ref[...] = x_ref[...] + _rms(y, gp_ref[...])


def _attn_out(x2, o_sb, o_dl, w, gain_post, *, tm):
    n, d = x2.shape
    row_spec = pl.BlockSpec((tm, d), lambda i: (i, 0))
    return pl.pallas_call(
        _attn_out_kernel,
        out_shape=jax.ShapeDtypeStruct((n, d), F32),
        grid=(n // tm,),
        in_specs=[
            row_spec,
            pl.BlockSpec((tm, o_sb.shape[1]), lambda i: (i, 0)),
            pl.BlockSpec((tm, o_dl.shape[1]), lambda i: (i, 0)),
            pl.BlockSpec(w.shape, lambda i: (0, 0), pipeline_mode=pl.Buffered(1)),
            pl.BlockSpec((1, d), lambda i: (0, 0)),
        ],
        out_specs=row_spec,
        compiler_params=_params(("parallel",)),
        name="attn_out",
    )(x2, o_sb, o_dl, w, gain_post)


def _ffn_kernel(x_ref, gpre_ref, wg_ref, wu_ref, wd_ref, gpost_ref, out_ref, h_ref, acc_ref):
    f = pl.program_id(1)

    @pl.when(f == 0)
    def _():
        h_ref[...] = _rms(x_ref[...], gpre_ref[...]).astype(BF16)
        acc_ref[...] = jnp.zeros_like(acc_ref)

    h = h_ref[...]
    g = jnp.dot(h, wg_ref[...], preferred_element_type=F32)
    u = jnp.dot(h, wu_ref[...], preferred_element_type=F32)
    act = (g * jax.nn.sigmoid(g) * u).astype(BF16)
    acc_ref[...] += jnp.dot(act, wd_ref[...], preferred_element_type=F32)

    @pl.when(f == pl.num_programs(1) - 1)
    def _():
        out_ref[...] = x_ref[...] + _rms(acc_ref[...], gpost_ref[...])


def _ffn(x2, gain_pre, w_gate_up, w_down, gain_post, *, tm, tf):
    n, d = x2.shape
    d_ff = w_down.shape[0]
    nf = d_ff // tf
    row_spec = pl.BlockSpec((tm, d), lambda i, f: (i, 0))
    vec_spec = pl.BlockSpec((1, d), lambda i, f: (0, 0))
    return pl.pallas_call(
        _ffn_kernel,
        out_shape=jax.ShapeDtypeStruct((n, d), F32),
        grid=(n // tm, nf),
        in_specs=[
            row_spec,
            vec_spec,
            pl.BlockSpec((d, tf), lambda i, f: (0, f)),
            pl.BlockSpec((d, tf), lambda i, f: (0, nf + f)),
            pl.BlockSpec((tf, d), lambda i, f: (f, 0)),
            vec_spec,
        ],
        out_specs=row_spec,
        scratch_shapes=[pltpu.VMEM((tm, d), BF16), pltpu.VMEM((tm, d), F32)],
        compiler_params=_params(("parallel", "arbitrary")),
        name="ffn",
    )(x2, gain_pre, w_gate_up, w_gate_up, w_down, gain_post)


def _pli_kernel(x_ref, p_ref, g_ref, wg_ref, wp_ref, out_ref):
    x = x_ref[...]
    h = _rms(x, g_ref[...]).astype(BF16)
    gate = jax.nn.sigmoid(jnp.dot(h, wg_ref[...], preferred_element_type=F32))
    emb = jnp.dot(p_ref[...].astype(BF16), wp_ref[...], preferred_element_type=F32)
    out_ref[...] = x + gate * emb


def _pli(x2, p2, gain, w_gate, w_proj, *, tm):
    n, d = x2.shape
    dp = p2.shape[1]
    row_spec = pl.BlockSpec((tm, d), lambda i: (i, 0))
    return pl.pallas_call(
        _pli_kernel,
        out_shape=jax.ShapeDtypeStruct((n, d), F32),
        grid=(n // tm,),
        in_specs=[
            row_spec,
            pl.BlockSpec((tm, dp), lambda i: (i, 0)),
            pl.BlockSpec((1, d), lambda i: (0, 0)),
            pl.BlockSpec(w_gate.shape, lambda i: (0, 0), pipeline_mode=pl.Buffered(1)),
            pl.BlockSpec(w_proj.shape, lambda i: (0, 0), pipeline_mode=pl.Buffered(1)),
        ],
        out_specs=row_spec,
        compiler_params=_params(("parallel",)),
        name="pli",
    )(x2, p2, gain, w_gate, w_proj)


def kernel(x, p, ln_mix_pre, w_in, ln_head, w_out, ln_mix_post, rel_bias, ln_ffn_pre, w_gate_up, w_down,
           ln_ffn_post, ln_pli, w_pli_gate, w_pli_proj):
    b, s, d = x.shape
    depth = w_in.shape[0]
    n = b * s
    assert d == N_HEADS * HEAD_DIM and w_in.shape[2] == 3 * d
    assert s % (BLOCK * DIL_GROUPS[-1][1]) == 0, "sequence must be a multiple of the widest dilated span"
    tq = min(1024, s)

    tables = _bias_tables(rel_bias)
    x2 = x.reshape(n, d)
    for i in range(depth):
        gain_heads = ln_head[i].reshape(N_HEADS, 1, HEAD_DIM)
        qkv_sb, qkv_dl = _qkv_proj(x2, ln_mix_pre[i][None], w_in[i].astype(BF16), tm=min(1024, n))
        o_sb = _sb_attention(qkv_sb.reshape(-1, b, s, HEAD_DIM), gain_heads, b, s, tq=tq, tk=256)
        o_dl = _dil_attention(qkv_dl.reshape(-1, b, s, HEAD_DIM), tables, gain_heads, b, s)
        x2 = _attn_out(x2, o_sb.reshape(n, -1), o_dl.reshape(n, -1), w_out[i].astype(BF16),
                       ln_mix_post[i][None], tm=min(512, n))
        x2 = _ffn(x2, ln_ffn_pre[i][None], w_gate_up[i].astype(BF16), w_down[i].astype(BF16),
                  ln_ffn_post[i][None], tm=min(512, n), tf=512)
        x2 = _pli(x2, p[i].reshape(n, -1), ln_pli[i][None], w_pli_gate[i].astype(BF16),
                  w_pli_proj[i].astype(BF16), tm=min(512, n))
    return x2.reshape(b, s, d)
```

```python
import functools
import math

import jax
import jax.numpy as jnp
import numpy as np
from jax import lax
from jax.experimental import pallas as pl
from jax.experimental.pallas import tpu as pltpu

HEAD_DIM = 128
N_HEADS_SB = 8
N_HEADS_DIL = 8
N_HEADS = N_HEADS_SB + N_HEADS_DIL
BLOCK = 128
DIL_GROUPS = ((128, 1), (512, 4), (2048, 16))
INTERLEAVE = 4
NUM_BUCKETS = 32
MAX_DISTANCE = 2048
RMS_EPS = 1e-6
MASK_VALUE = -1e30

VMEM_LIMIT_BYTES = 56 * 1024 * 1024

F32 = jnp.float32
BF16 = jnp.bfloat16
LOG2E = math.log2(math.e)
SATURATED = 151.0
_NT = (((1,), (1,)), ((), ()))


def _params(semantics):
    return pltpu.CompilerParams(dimension_semantics=semantics, vmem_limit_bytes=VMEM_LIMIT_BYTES)


def _rms(x, gain):
    ms = jnp.mean(x * x, axis=-1, keepdims=True)
    return x * lax.rsqrt(ms + RMS_EPS) * gain


def _qkv_kernel(x_ref, g_ref, w_ref, sb_ref, dl_ref, h_ref, *, n_sb_steps):
    j = pl.program_id(1)

    @pl.when(j == 0)
    def _():
        h_ref[...] = _rms(x_ref[...], g_ref[...]).astype(BF16)

    r = jnp.dot(h_ref[...], w_ref[...], preferred_element_type=F32)

    @pl.when(j < n_sb_steps)
    def _():
        for hh in range(sb_ref.shape[0]):
            sb_ref[hh] = r[:, hh * HEAD_DIM:(hh + 1) * HEAD_DIM].astype(BF16)

    @pl.when(j >= n_sb_steps)
    def _():
        for hh in range(dl_ref.shape[0]):
            dl_ref[hh] = r[:, hh * HEAD_DIM:(hh + 1) * HEAD_DIM]


def _qkv_proj(x2, gain, w, *, tm):
    n, d = x2.shape
    heads_per_step = 8
    tn = heads_per_step * HEAD_DIM
    n_sb_steps = 3 * N_HEADS_SB // heads_per_step
    n_dl_steps = 3 * N_HEADS_DIL // heads_per_step
    assert w.shape[1] == (n_sb_steps + n_dl_steps) * tn
    return pl.pallas_call(
        functools.partial(_qkv_kernel, n_sb_steps=n_sb_steps),
        out_shape=[
            jax.ShapeDtypeStruct((3 * N_HEADS_SB, n, HEAD_DIM), BF16),
            jax.ShapeDtypeStruct((3 * N_HEADS_DIL, n, HEAD_DIM), F32),
        ],
        grid=(n // tm, n_sb_steps + n_dl_steps),
        in_specs=[
            pl.BlockSpec((tm, d), lambda i, j: (i, 0)),
            pl.BlockSpec((1, d), lambda i, j: (0, 0)),
            pl.BlockSpec((d, tn), lambda i, j: (0, j)),
        ],
        out_specs=[
            pl.BlockSpec((heads_per_step, tm, HEAD_DIM),
                         lambda i, j: (jnp.minimum(j, n_sb_steps - 1), i, 0)),
            pl.BlockSpec((heads_per_step, tm, HEAD_DIM),
                         lambda i, j: (jnp.maximum(j - n_sb_steps, 0), i, 0)),
        ],
        scratch_shapes=[pltpu.VMEM((tm, d), BF16)],
        compiler_params=_params(("parallel", "arbitrary")),
        name="qkv_proj",
    )(x2, gain, w)


def _sb_kernel(q_ref, k_ref, v_ref, g_ref, o_ref, u_ref, c_ref, acc_ref, *, tq, tk, scale):
    i = pl.program_id(2)
    nsub = tq // tk
    row_u = lax.broadcasted_iota(jnp.int32, (tk, tk), 0)
    col_u = lax.broadcasted_iota(jnp.int32, (tk, tk), 1)
    u_ref[...] = jnp.where(row_u >= col_u, 1.0, 0.0).astype(BF16)
    earlier = col_u < row_u

    def logits(r0, kstart):
        k = k_ref[pl.ds(kstart, tk), :]
        z2 = lax.dot_general(q_ref[r0:, :], k, _NT, preferred_element_type=F32) * (scale * LOG2E)
        neg_abs = pltpu.bitcast(pltpu.bitcast(z2, jnp.uint32) | jnp.uint32(0x80000000), F32)
        sp2 = jnp.maximum(z2, 0.0) + jnp.log2(1.0 + jnp.exp2(neg_abs))
        return z2, sp2

    def split(sp2):
        hi = sp2.astype(BF16)
        return hi, (sp2 - hi.astype(F32)).astype(BF16)

    def mask_top(x):
        top = jnp.where(earlier, x[:tk], 0.0)
        return top if x.shape[0] == tk else jnp.concatenate([top, x[tk:]], axis=0)

    def accumulate(r0, kstart, z2, hi, lo, diagonal):
        u = u_ref[...]
        incl = (jnp.dot(hi, u, preferred_element_type=F32)
                + jnp.dot(lo, u, preferred_element_type=F32))
        c = c_ref[r0:, :]
        a = jnp.exp2(z2 - incl - c)
        if diagonal:
            a = mask_top(a)
        v = v_ref[pl.ds(kstart, tk), :]
        acc_ref[r0:, :] += jnp.dot(a.astype(BF16), v, preferred_element_type=F32)
        c_new = c + incl[:, 0:1]
        c_ref[r0:, :] = c_new
        return c_new

    c_ref[...] = jnp.zeros_like(c_ref)
    acc_ref[...] = jnp.zeros_like(acc_ref)

    diag = []
    for jj in reversed(range(nsub)):
        kstart = pl.multiple_of(i * tq + jj * tk, tk)
        z2, sp2 = logits(jj * tk, kstart)
        diag.append((jj * tk, kstart, z2) + split(mask_top(sp2)))
    for r0, kstart, z2, hi, lo in diag:
        accumulate(r0, kstart, z2, hi, lo, True)

    n = i * nsub

    def left_tile(state):
        s, _ = state
        kstart = pl.multiple_of((n - 1 - s) * tk, tk)
        z2, sp2 = logits(0, kstart)
        c_new = accumulate(0, kstart, z2, *split(sp2), False)
        return s + 1, (jnp.min(c_new) >= SATURATED).astype(jnp.int32)

    lax.while_loop(lambda state: (state[0] < n) & (state[1] == 0), left_tile, (jnp.int32(0), jnp.int32(0)))
    o_ref[...] = _rms(acc_ref[...], g_ref[...]).astype(BF16)


def _sb_attention(qkv, gain_heads, b, s, *, tq, tk):
    kern = functools.partial(_sb_kernel, tq=tq, tk=tk, scale=1.0 / math.sqrt(HEAD_DIM))
    return pl.pallas_call(
        kern,
        out_shape=jax.ShapeDtypeStruct((b, s, N_HEADS_SB * HEAD_DIM), BF16),
        grid=(b, N_HEADS_SB, s // tq),
        in_specs=[
            pl.BlockSpec((None, None, tq, HEAD_DIM), lambda bi, h, i: (h, bi, i, 0)),
            pl.BlockSpec((None, None, s, HEAD_DIM), lambda bi, h, i: (N_HEADS_SB + h, bi, 0, 0)),
            pl.BlockSpec((None, None, s, HEAD_DIM), lambda bi, h, i: (2 * N_HEADS_SB + h, bi, 0, 0)),
            pl.BlockSpec((None, 1, HEAD_DIM), lambda bi, h, i: (h, 0, 0)),
        ],
        out_specs=pl.BlockSpec((None, tq, HEAD_DIM), lambda bi, h, i: (bi, i, h)),
        scratch_shapes=[
            pltpu.VMEM((tk, tk), BF16),
            pltpu.VMEM((tq, 1), F32),
            pltpu.VMEM((tq, HEAD_DIM), F32),
        ],
        compiler_params=_params(("parallel", "parallel", "arbitrary")),
        name="sb_attn",
    )(qkv, qkv, qkv, gain_heads)


def _t5_bucket_np(dist):
    max_exact = NUM_BUCKETS // 2
    d = np.maximum(dist, 1).astype(np.float64)
    large = max_exact + (np.log(d / max_exact) / math.log(MAX_DISTANCE / max_exact)
                         * (NUM_BUCKETS - max_exact)).astype(np.int64)
    large = np.minimum(large, NUM_BUCKETS - 1)
    return np.where(dist < max_exact, dist, large)


def _bias_tables(rel_bias):
    qi = np.arange(BLOCK)[:, None]
    ki = np.arange(2 * BLOCK)[None, :]
    rel = BLOCK + qi - ki
    band = (rel >= 0) & (rel <= BLOCK)
    bias_by_bucket = rel_bias.astype(F32)
    tables = []
    for _, dilation in DIL_GROUPS:
        bucket = _t5_bucket_np(np.maximum(rel, 0) * dilation)
        table = jnp.full((bias_by_bucket.shape[1], BLOCK, 2 * BLOCK), MASK_VALUE, F32)
        for bkt in np.unique(bucket[band]):
            table = jnp.where((band & (bucket == bkt))[None], bias_by_bucket[bkt][:, None, None], table)
        tables.append(table)
    return jnp.stack(tables, axis=0)


def _dil_kernel(tbl_ref, q_ref, k_ref, v_ref, g_ref, o_ref, t_ref, og_ref, lg_ref, *, seq, scale, tiles_per_step):
    quarter = seq // INTERLEAVE
    for x, ref in enumerate((q_ref, k_ref, v_ref)):
        for a in range(INTERLEAVE):
            t_ref[x, a * quarter:(a + 1) * quarter, :] = ref[pl.ds(a, quarter, stride=INTERLEAVE), :]

    def rows(start, count, stride):
        if stride == 1:
            return pl.ds(start if isinstance(start, int) else pl.multiple_of(start, BLOCK), count)
        return pl.ds(start, count, stride=stride)

    sources = (
        (lambda x, rws: (q_ref, k_ref, v_ref)[x][rws, :], 1, 1, lambda r: 0),
        (lambda x, rws: t_ref[x, rws, :], 1, INTERLEAVE, lambda r: r * quarter),
        (lambda x, rws: t_ref[x, rws, :], INTERLEAVE, INTERLEAVE * INTERLEAVE,
         lambda r: (r % INTERLEAVE) * quarter + r // INTERLEAVE),
    )

    def emit(g, stride, tiles):
        logits = [lax.dot_general(q, k2, _NT, preferred_element_type=F32) * scale + tbl
                  for q, k2, _, tbl, _ in tiles]
        stats = []
        for s in logits:
            m = jnp.max(s, axis=-1, keepdims=True)
            p = jnp.exp(s - m)
            stats.append((m, p.astype(BF16), jnp.sum(p, axis=-1, keepdims=True)))
        for (m, p, denom), (_, _, v2, _, start) in zip(stats, tiles):
            o = jnp.dot(p, v2, preferred_element_type=F32) / denom
            og_ref[g, rows(start, BLOCK, stride), :] = o
            lg_ref[g, rows(start, BLOCK, stride), :] = jnp.broadcast_to(m + jnp.log(denom), (BLOCK, HEAD_DIM))

    for g, (_, d) in enumerate(DIL_GROUPS):
        src, stride, n_sub, base_of = sources[g]
        assert n_sub == d
        nb = seq // (BLOCK * d)
        nblk = min(tiles_per_step, nb)
        span = BLOCK * stride

        def load(x, start, count, src=src, stride=stride):
            return src(x, rows(start, count, stride)).astype(BF16)

        def first(base, g=g, nblk=nblk, span=span, load=load):
            q = load(0, base, nblk * BLOCK)
            k = load(1, base, nblk * BLOCK)
            v = load(2, base, nblk * BLOCK)
            tiles = [(q[:BLOCK], k[:BLOCK], v[:BLOCK], tbl_ref[g, :, BLOCK:], base)]
            for m in range(1, nblk):
                tiles.append((q[m * BLOCK:(m + 1) * BLOCK], k[(m - 1) * BLOCK:(m + 1) * BLOCK],
                              v[(m - 1) * BLOCK:(m + 1) * BLOCK], tbl_ref[g], base + m * span))
            return tiles

        def later(base, n0, g=g, nblk=nblk, span=span, load=load):
            q = load(0, base + n0 * span, nblk * BLOCK)
            k = load(1, base + (n0 - 1) * span, (nblk + 1) * BLOCK)
            v = load(2, base + (n0 - 1) * span, (nblk + 1) * BLOCK)
            return [(q[m * BLOCK:(m + 1) * BLOCK], k[m * BLOCK:(m + 2) * BLOCK],
                     v[m * BLOCK:(m + 2) * BLOCK], tbl_ref[g], base + (n0 + m) * span) for m in range(nblk)]

        unroll = max(1, min(d, tiles_per_step // nb))

        def r_body(t, carry, g=g, stride=stride, nb=nb, nblk=nblk, unroll=unroll, first=first, later=later,
                   base_of=base_of):
            bases = [base_of(t * unroll + j) for j in range(unroll)]
            emit(g, stride, [tl for base in bases for tl in first(base)])
            if nb > nblk:
                def body(u, carry2):
                    emit(g, stride, [tl for base in bases for tl in later(base, u * nblk)])
                    return carry2
                lax.fori_loop(1, nb // nblk, body, 0)
            return carry

        if d == unroll:
            r_body(0, 0)
        else:
            lax.fori_loop(0, d // unroll, r_body, 0)

    gain = g_ref[...]

    def finish(a, carry):
        seq_rows = pl.ds(a, quarter, stride=INTERLEAVE)
        t_rows = pl.ds(pl.multiple_of(a * quarter, BLOCK), quarter)
        l0, l1, l2 = lg_ref[0, seq_rows, :], lg_ref[1, t_rows, :], lg_ref[2, t_rows, :]
        m = jnp.maximum(jnp.maximum(l0, l1), l2)
        e0, e1, e2 = jnp.exp(l0 - m), jnp.exp(l1 - m), jnp.exp(l2 - m)
        o = ((e0 * og_ref[0, seq_rows, :] + e1 * og_ref[1, t_rows, :] + e2 * og_ref[2, t_rows, :])
             / (e0 + e1 + e2))
        o_ref[seq_rows, :] = _rms(o, gain)
        return carry

    lax.fori_loop(0, INTERLEAVE, finish, 0)


def _dil_attention(qkv, tables, gain_heads, b, s):
    n_groups = len(DIL_GROUPS)
    assert tuple(d for _, d in DIL_GROUPS) == (1, INTERLEAVE, INTERLEAVE * INTERLEAVE)
    kern = functools.partial(_dil_kernel, seq=s, scale=1.0 / math.sqrt(HEAD_DIM), tiles_per_step=16)
    return pl.pallas_call(
        kern,
        out_shape=jax.ShapeDtypeStruct((b, s, N_HEADS_DIL * HEAD_DIM), F32),
        grid=(b, N_HEADS_DIL),
        in_specs=[
            pl.BlockSpec((n_groups, None, BLOCK, 2 * BLOCK), lambda bi, h: (0, h, 0, 0)),
            pl.BlockSpec((None, None, s, HEAD_DIM), lambda bi, h: (h, bi, 0, 0)),
            pl.BlockSpec((None, None, s, HEAD_DIM), lambda bi, h: (N_HEADS_DIL + h, bi, 0, 0)),
            pl.BlockSpec((None, None, s, HEAD_DIM), lambda bi, h: (2 * N_HEADS_DIL + h, bi, 0, 0)),
            pl.BlockSpec((None, 1, HEAD_DIM), lambda bi, h: (N_HEADS_SB + h, 0, 0)),
        ],
        out_specs=pl.BlockSpec((None, s, HEAD_DIM), lambda bi, h: (bi, 0, h)),
        scratch_shapes=[
            pltpu.VMEM((3, s, HEAD_DIM), F32),
            pltpu.VMEM((n_groups, s, HEAD_DIM), F32),
            pltpu.VMEM((n_groups, s, HEAD_DIM), F32),
        ],
        compiler_params=_params(("parallel", "parallel")),
        name="dil_attn",
    )(tables, qkv, qkv, qkv, gain_heads)


def _attn_out_kernel(x_ref, osb_ref, odl_ref, w_ref, gp_ref, out_ref):
    k_sb = osb_ref.shape[1]
    y = (jnp.dot(osb_ref[...], w_ref[:k_sb, :], preferred_element_type=F32)
         + jnp.dot(odl_ref[...].astype(BF16), w_ref[k_sb:, :], preferred_element_type=F32))
    out_ref[...] = x_ref[...] + _rms(y, gp_ref[...])


def _attn_out(x2, o_sb, o_dl, w, gain_post, *, tm):
    n, d = x2.shape
    row_spec = pl.BlockSpec((tm, d), lambda i: (i, 0))
    return pl.pallas_call(
        _attn_out_kernel,
        out_shape=jax.ShapeDtypeStruct((n, d), F32),
        grid=(n // tm,),
        in_specs=[
            row_spec,
            pl.BlockSpec((tm, o_sb.shape[1]), lambda i: (i, 0)),
            pl.BlockSpec((tm, o_dl.shape[1]), lambda i: (i, 0)),
            pl.BlockSpec(w.shape, lambda i: (0, 0), pipeline_mode=pl.Buffered(1)),
            pl.BlockSpec((1, d), lambda i: (0, 0)),
        ],
        out_specs=row_spec,
        compiler_params=_params(("parallel",)),
        name="attn_out",
    )(x2, o_sb, o_dl, w, gain_post)


def _ffn_kernel(x_ref, gpre_ref, wg_ref, wu_ref, wd_ref, gpost_ref, out_ref, h_ref, acc_ref):
    f = pl.program_id(1)

    @pl.when(f == 0)
    def _():
        h_ref[...] = _rms(x_ref[...], gpre_ref[...]).astype(BF16)
        acc_ref[...] = jnp.zeros_like(acc_ref)

    h = h_ref[...]
    g = jnp.dot(h, wg_ref[...], preferred_element_type=F32)
    u = jnp.dot(h, wu_ref[...], preferred_element_type=F32)
    act = (g * jax.nn.sigmoid(g) * u).astype(BF16)
    acc_ref[...] += jnp.dot(act, wd_ref[...], preferred_element_type=F32)

    @pl.when(f == pl.num_programs(1) - 1)
    def _():
        out_ref[...] = x_ref[...] + _rms(acc_ref[...], gpost_ref[...])


def _ffn(x2, gain_pre, w_gate_up, w_down, gain_post, *, tm, tf):
    n, d = x2.shape
    d_ff = w_down.shape[0]
    nf = d_ff // tf
    row_spec = pl.BlockSpec((tm, d), lambda i, f: (i, 0))
    vec_spec = pl.BlockSpec((1, d), lambda i, f: (0, 0))
    return pl.pallas_call(
        _ffn_kernel,
        out_shape=jax.ShapeDtypeStruct((n, d), F32),
        grid=(n // tm, nf),
        in_specs=[
            row_spec,
            vec_spec,
            pl.BlockSpec((d, tf), lambda i, f: (0, f)),
            pl.BlockSpec((d, tf), lambda i, f: (0, nf + f)),
            pl.BlockSpec((tf, d), lambda i, f: (f, 0)),
            vec_spec,
        ],
        out_specs=row_spec,
        scratch_shapes=[pltpu.VMEM((tm, d), BF16), pltpu.VMEM((tm, d), F32)],
        compiler_params=_params(("parallel", "arbitrary")),
        name="ffn",
    )(x2, gain_pre, w_gate_up, w_gate_up, w_down, gain_post)


def _pli_kernel(x_ref, p_ref, g_ref, wg_ref, wp_ref, out_ref):
    x = x_ref[...]
    h = _rms(x, g_ref[...]).astype(BF16)
    gate = jax.nn.sigmoid(jnp.dot(h, wg_ref[...], preferred_element_type=F32))
    emb = jnp.dot(p_ref[...].astype(BF16), wp_ref[...], preferred_element_type=F32)
    out_ref[...] = x + gate * emb


def _pli(x2, p2, gain, w_gate, w_proj, *, tm):
    n, d = x2.shape
    dp = p2.shape[1]
    row_spec = pl.BlockSpec((tm, d), lambda i: (i, 0))
    return pl.pallas_call(
        _pli_kernel,
        out_shape=jax.ShapeDtypeStruct((n, d), F32),
        grid=(n // tm,),
        in_specs=[
            row_spec,
            pl.BlockSpec((tm, dp), lambda i: (i, 0)),
            pl.BlockSpec((1, d), lambda i: (0, 0)),
            pl.BlockSpec(w_gate.shape, lambda i: (0, 0), pipeline_mode=pl.Buffered(1)),
            pl.BlockSpec(w_proj.shape, lambda i: (0, 0), pipeline_mode=pl.Buffered(1)),
        ],
        out_specs=row_spec,
        compiler_params=_params(("parallel",)),
        name="pli",
    )(x2, p2, gain, w_gate, w_proj)


def kernel(x, p, ln_mix_pre, w_in, ln_head, w_out, ln_mix_post, rel_bias, ln_ffn_pre, w_gate_up, w_down,
           ln_ffn_post, ln_pli, w_pli_gate, w_pli_proj):
    b, s, d = x.shape
    depth = w_in.shape[0]
    n = b * s
    assert d == N_HEADS * HEAD_DIM and w_in.shape[2] == 3 * d
    assert s % (BLOCK * DIL_GROUPS[-1][1]) == 0, "sequence must be a multiple of the widest dilated span"
    tq = min(1024, s)

    tables = _bias_tables(rel_bias)
    x2 = x.reshape(n, d)
    for i in range(depth):
        gain_heads = ln_head[i].reshape(N_HEADS, 1, HEAD_DIM)
        qkv_sb, qkv_dl = _qkv_proj(x2, ln_mix_pre[i][None], w_in[i].astype(BF16), tm=min(1024, n))
        o_sb = _sb_attention(qkv_sb.reshape(-1, b, s, HEAD_DIM), gain_heads, b, s, tq=tq, tk=256)
        o_dl = _dil_attention(qkv_dl.reshape(-1, b, s, HEAD_DIM), tables, gain_heads, b, s)
        x2 = _attn_out(x2, o_sb.reshape(n, -1), o_dl.reshape(n, -1), w_out[i].astype(BF16),
                       ln_mix_post[i][None], tm=min(512, n))
        x2 = _ffn(x2, ln_ffn_pre[i][None], w_gate_up[i].astype(BF16), w_down[i].astype(BF16),
                  ln_ffn_post[i][None], tm=min(512, n), tf=512)
        x2 = _pli(x2, p[i].reshape(n, -1), ln_pli[i][None], w_pli_gate[i].astype(BF16),
                  w_pli_proj[i].astype(BF16), tm=min(512, n))
    return x2.reshape(b, s, d)
```

```python
import functools
import math
from typing import NamedTuple

import jax
import jax.numpy as jnp
import numpy as np
from jax import lax
from jax.experimental import pallas as pl
from jax.experimental.pallas import tpu as pltpu

HEAD_DIM = 128
N_HEADS_SB = 8
N_HEADS_DIL = 8
N_HEADS = N_HEADS_SB + N_HEADS_DIL
BLOCK = 128
DIL_GROUPS = ((128, 1), (512, 4), (2048, 16))
INTERLEAVE = 4
NUM_BUCKETS = 32
MAX_DISTANCE = 2048
RMS_EPS = 1e-6
MASK_VALUE = -1e30

VMEM_LIMIT_BYTES = 56 * 1024 * 1024

F32 = jnp.float32
BF16 = jnp.bfloat16
LOG2E = math.log2(math.e)
SATURATED = 151.0
_NT = (((1,), (1,)), ((), ()))


def _params(semantics):
    return pltpu.CompilerParams(dimension_semantics=semantics, vmem_limit_bytes=VMEM_LIMIT_BYTES)


def _rms(x, gain):
    ms = jnp.mean(x * x, axis=-1, keepdims=True)
    return x * lax.rsqrt(ms + RMS_EPS) * gain


def _qkv_kernel(x_ref, g_ref, w_ref, sb_ref, dl_ref, h_ref, *, n_sb_steps):
    j = pl.program_id(1)

    @pl.when(j == 0)
    def _():
        h_ref[...] = _rms(x_ref[...], g_ref[...]).astype(BF16)

    r = jnp.dot(h_ref[...], w_ref[...], preferred_element_type=F32)

    @pl.when(j < n_sb_steps)
    def _():
        for hh in range(sb_ref.shape[0]):
            sb_ref[hh] = r[:, hh * HEAD_DIM:(hh + 1) * HEAD_DIM].astype(BF16)

    @pl.when(j >= n_sb_steps)
    def _():
        for hh in range(dl_ref.shape[0]):
            dl_ref[hh] = r[:, hh * HEAD_DIM:(hh + 1) * HEAD_DIM]


def _qkv_proj(x2, gain, w, *, tm):
    n, d = x2.shape
    heads_per_step = 8
    tn = heads_per_step * HEAD_DIM
    n_sb_steps = 3 * N_HEADS_SB // heads_per_step
    n_dl_steps = 3 * N_HEADS_DIL // heads_per_step
    assert w.shape[1] == (n_sb_steps + n_dl_steps) * tn
    return pl.pallas_call(
        functools.partial(_qkv_kernel, n_sb_steps=n_sb_steps),
        out_shape=[
            jax.ShapeDtypeStruct((3 * N_HEADS_SB, n, HEAD_DIM), BF16),
            jax.ShapeDtypeStruct((3 * N_HEADS_DIL, n, HEAD_DIM), F32),
        ],
        grid=(n // tm, n_sb_steps + n_dl_steps),
        in_specs=[
            pl.BlockSpec((tm, d), lambda i, j: (i, 0)),
            pl.BlockSpec((1, d), lambda i, j: (0, 0)),
            pl.BlockSpec((d, tn), lambda i, j: (0, j)),
        ],
        out_specs=[
            pl.BlockSpec((heads_per_step, tm, HEAD_DIM),
                         lambda i, j: (jnp.minimum(j, n_sb_steps - 1), i, 0)),
            pl.BlockSpec((heads_per_step, tm, HEAD_DIM),
                         lambda i, j: (jnp.maximum(j - n_sb_steps, 0), i, 0)),
        ],
        scratch_shapes=[pltpu.VMEM((tm, d), BF16)],
        compiler_params=_params(("parallel", "arbitrary")),
        name="qkv_proj",
    )(x2, gain, w)


def _sb_kernel(q_ref, k_ref, v_ref, g_ref, o_ref, u_ref, c_ref, acc_ref, *, tq, tk, scale):
    i = pl.program_id(2)
    nsub = tq // tk
    row_u = lax.broadcasted_iota(jnp.int32, (tk, tk), 0)
    col_u = lax.broadcasted_iota(jnp.int32, (tk, tk), 1)
    u_ref[...] = jnp.where(col_u >= row_u, 1.0, 0.0).astype(BF16)
    earlier = row_u < col_u

    def mask_first(x):
        first = jnp.where(earlier, x[:, :tk], 0.0)
        return first if x.shape[1] == tk else jnp.concatenate([first, x[:, tk:]], axis=1)

    def visit(tiles):
        staged = []
        for r0, kstart, diagonal in tiles:
            k = k_ref[pl.ds(kstart, tk), :]
            z2 = lax.dot_general(k, q_ref[r0:, :], _NT, preferred_element_type=F32) * (scale * LOG2E)
            neg_abs = pltpu.bitcast(pltpu.bitcast(z2, jnp.uint32) | jnp.uint32(0x80000000), F32)
            sp2 = jnp.maximum(z2, 0.0) + jnp.log2(1.0 + jnp.exp2(neg_abs))
            if diagonal:
                sp2 = mask_first(sp2)
            hi = sp2.astype(BF16)
            staged.append((z2, hi, (sp2 - hi.astype(F32)).astype(BF16)))
        u = u_ref[...]
        incls = [jnp.dot(u, hi, preferred_element_type=F32) + jnp.dot(u, lo, preferred_element_type=F32)
                 for _, hi, lo in staged]
        c = c_ref[...]
        before = []
        for (r0, _, _), incl in zip(tiles, incls):
            before.append(c[:, r0:])
            grown = c[:, r0:] + incl[0:1, :]
            c = grown if r0 == 0 else jnp.concatenate([c[:, :r0], grown], axis=1)
        c_ref[...] = c
        for (r0, kstart, diagonal), (z2, _, _), incl, c0 in zip(tiles, staged, incls, before):
            a = jnp.exp2(z2 - incl - c0)
            if diagonal:
                a = mask_first(a)
            v = v_ref[pl.ds(kstart, tk), :]
            acc_ref[:, r0:] += lax.dot_general(v, a.astype(BF16), (((0,), (0,)), ((), ())),
                                               preferred_element_type=F32)
        return c

    c_ref[...] = jnp.zeros_like(c_ref)
    acc_ref[...] = jnp.zeros_like(acc_ref)

    diag = [(jj * tk, pl.multiple_of(i * tq + jj * tk, tk), True) for jj in reversed(range(nsub))]
    n = i * nsub

    def left(s):
        return (0, pl.multiple_of(jnp.maximum(n - 1 - s, 0) * tk, tk), False)

    def saturated(c):
        return (jnp.min(c) >= SATURATED).astype(jnp.int32)

    done = lax.cond(n > 0, lambda: saturated(visit(diag + [left(0)])), lambda: saturated(visit(diag)))

    def step(state):
        s, _ = state
        return s + 1, saturated(visit([left(s)]))

    lax.while_loop(lambda state: (state[0] < n) & (state[1] == 0), step, (jnp.int32(1), done))
    o_ref[...] = _rms(acc_ref[...].T, g_ref[...]).astype(BF16)


def _sb_attention(qkv, gain_heads, b, s, *, tq, tk):
    kern = functools.partial(_sb_kernel, tq=tq, tk=tk, scale=1.0 / math.sqrt(HEAD_DIM))
    return pl.pallas_call(
        kern,
        out_shape=jax.ShapeDtypeStruct((b, s, N_HEADS_SB * HEAD_DIM), BF16),
        grid=(b, N_HEADS_SB, s // tq),
        in_specs=[
            pl.BlockSpec((None, None, tq, HEAD_DIM), lambda bi, h, i: (h, bi, i, 0)),
            pl.BlockSpec((None, None, s, HEAD_DIM), lambda bi, h, i: (N_HEADS_SB + h, bi, 0, 0)),
            pl.BlockSpec((None, None, s, HEAD_DIM), lambda bi, h, i: (2 * N_HEADS_SB + h, bi, 0, 0)),
            pl.BlockSpec((None, 1, HEAD_DIM), lambda bi, h, i: (h, 0, 0)),
        ],
        out_specs=pl.BlockSpec((None, tq, HEAD_DIM), lambda bi, h, i: (bi, i, h)),
        scratch_shapes=[
            pltpu.VMEM((tk, tk), BF16),
            pltpu.VMEM((1, tq), F32),
            pltpu.VMEM((HEAD_DIM, tq), F32),
        ],
        compiler_params=_params(("parallel", "parallel", "arbitrary")),
        name="sb_attn",
    )(qkv, qkv, qkv, gain_heads)


def _t5_bucket_np(dist):
    max_exact = NUM_BUCKETS // 2
    d = np.maximum(dist, 1).astype(np.float64)
    large = max_exact + (np.log(d / max_exact) / math.log(MAX_DISTANCE / max_exact)
                         * (NUM_BUCKETS - max_exact)).astype(np.int64)
    large = np.minimum(large, NUM_BUCKETS - 1)
    return np.where(dist < max_exact, dist, large)


def _bias_tables(rel_bias):
    qi = np.arange(BLOCK)[:, None]
    ki = np.arange(2 * BLOCK)[None, :]
    rel = BLOCK + qi - ki
    band = (rel >= 0) & (rel <= BLOCK)
    bias_by_bucket = rel_bias.astype(F32)
    tables = []
    for _, dilation in DIL_GROUPS:
        bucket = _t5_bucket_np(np.maximum(rel, 0) * dilation)
        table = jnp.full((bias_by_bucket.shape[1], BLOCK, 2 * BLOCK), MASK_VALUE, F32)
        for bkt in np.unique(bucket[band]):
            table = jnp.where((band & (bucket == bkt))[None], bias_by_bucket[bkt][:, None, None], table)
        tables.append(table)
    return jnp.stack(tables, axis=0)


def _dil_kernel(tbl_ref, q_ref, k_ref, v_ref, g_ref, o_ref, t_ref, og_ref, lg_ref, *, seq, scale, tiles_per_step):
    quarter = seq // INTERLEAVE
    for x, ref in enumerate((q_ref, k_ref, v_ref)):
        for a in range(INTERLEAVE):
            t_ref[x, a * quarter:(a + 1) * quarter, :] = ref[pl.ds(a, quarter, stride=INTERLEAVE), :]

    def rows(start, count, stride):
        if stride == 1:
            return pl.ds(start if isinstance(start, int) else pl.multiple_of(start, BLOCK), count)
        return pl.ds(start, count, stride=stride)

    sources = (
        (lambda x, rws: (q_ref, k_ref, v_ref)[x][rws, :], 1, 1, lambda r: 0),
        (lambda x, rws: t_ref[x, rws, :], 1, INTERLEAVE, lambda r: r * quarter),
        (lambda x, rws: t_ref[x, rws, :], INTERLEAVE, INTERLEAVE * INTERLEAVE,
         lambda r: (r % INTERLEAVE) * quarter + r // INTERLEAVE),
    )

    def emit(g, stride, tiles):
        logits = [lax.dot_general(q, k2, _NT, preferred_element_type=F32) * scale + tbl
                  for q, k2, _, tbl, _ in tiles]
        stats = []
        for s in logits:
            m = jnp.max(s, axis=-1, keepdims=True)
            p = jnp.exp(s - m)
            stats.append((m, p.astype(BF16), jnp.sum(p, axis=-1, keepdims=True)))
        for (m, p, denom), (_, _, v2, _, start) in zip(stats, tiles):
            o = jnp.dot(p, v2, preferred_element_type=F32) / denom
            og_ref[g, rows(start, BLOCK, stride), :] = o
            lg_ref[g, rows(start, BLOCK, stride), :] = jnp.broadcast_to(m + jnp.log(denom), (BLOCK, HEAD_DIM))

    for g, (_, d) in enumerate(DIL_GROUPS):
        src, stride, n_sub, base_of = sources[g]
        assert n_sub == d
        nb = seq // (BLOCK * d)
        nblk = min(tiles_per_step, nb)
        span = BLOCK * stride

        def load(x, start, count, src=src, stride=stride):
            return src(x, rows(start, count, stride)).astype(BF16)

        def first(base, g=g, nblk=nblk, span=span, load=load):
            q = load(0, base, nblk * BLOCK)
            k = load(1, base, nblk * BLOCK)
            v = load(2, base, nblk * BLOCK)
            tiles = [(q[:BLOCK], k[:BLOCK], v[:BLOCK], tbl_ref[g, :, BLOCK:], base)]
            for m in range(1, nblk):
                tiles.append((q[m * BLOCK:(m + 1) * BLOCK], k[(m - 1) * BLOCK:(m + 1) * BLOCK],
                              v[(m - 1) * BLOCK:(m + 1) * BLOCK], tbl_ref[g], base + m * span))
            return tiles

        def later(base, n0, g=g, nblk=nblk, span=span, load=load):
            q = load(0, base + n0 * span, nblk * BLOCK)
            k = load(1, base + (n0 - 1) * span, (nblk + 1) * BLOCK)
            v = load(2, base + (n0 - 1) * span, (nblk + 1) * BLOCK)
            return [(q[m * BLOCK:(m + 1) * BLOCK], k[m * BLOCK:(m + 2) * BLOCK],
                     v[m * BLOCK:(m + 2) * BLOCK], tbl_ref[g], base + (n0 + m) * span) for m in range(nblk)]

        unroll = max(1, min(d, tiles_per_step // nb))

        def r_body(t, carry, g=g, stride=stride, nb=nb, nblk=nblk, unroll=unroll, first=first, later=later,
                   base_of=base_of):
            bases = [base_of(t * unroll + j) for j in range(unroll)]
            emit(g, stride, [tl for base in bases for tl in first(base)])
            if nb > nblk:
                def body(u, carry2):
                    emit(g, stride, [tl for base in bases for tl in later(base, u * nblk)])
                    return carry2
                lax.fori_loop(1, nb // nblk, body, 0)
            return carry

        if d == unroll:
            r_body(0, 0)
        else:
            lax.fori_loop(0, d // unroll, r_body, 0)

    gain = g_ref[...]

    def finish(a, carry):
        seq_rows = pl.ds(a, quarter, stride=INTERLEAVE)
        t_rows = pl.ds(pl.multiple_of(a * quarter, BLOCK), quarter)
        l0, l1, l2 = lg_ref[0, seq_rows, :], lg_ref[1, t_rows, :], lg_ref[2, t_rows, :]
        m = jnp.maximum(jnp.maximum(l0, l1), l2)
        e0, e1, e2 = jnp.exp(l0 - m), jnp.exp(l1 - m), jnp.exp(l2 - m)
        o = ((e0 * og_ref[0, seq_rows, :] + e1 * og_ref[1, t_rows, :] + e2 * og_ref[2, t_rows, :])
             / (e0 + e1 + e2))
        o_ref[seq_rows, :] = _rms(o, gain)
        return carry

    lax.fori_loop(0, INTERLEAVE, finish, 0)


def _dil_attention(qkv, tables, gain_heads, b, s):
    n_groups = len(DIL_GROUPS)
    assert tuple(d for _, d in DIL_GROUPS) == (1, INTERLEAVE, INTERLEAVE * INTERLEAVE)
    kern = functools.partial(_dil_kernel, seq=s, scale=1.0 / math.sqrt(HEAD_DIM), tiles_per_step=16)
    return pl.pallas_call(
        kern,
        out_shape=jax.ShapeDtypeStruct((b, s, N_HEADS_DIL * HEAD_DIM), F32),
        grid=(b, N_HEADS_DIL),
        in_specs=[
            pl.BlockSpec((n_groups, None, BLOCK, 2 * BLOCK), lambda bi, h: (0, h, 0, 0)),
            pl.BlockSpec((None, None, s, HEAD_DIM), lambda bi, h: (h, bi, 0, 0)),
            pl.BlockSpec((None, None, s, HEAD_DIM), lambda bi, h: (N_HEADS_DIL + h, bi, 0, 0)),
            pl.BlockSpec((None, None, s, HEAD_DIM), lambda bi, h: (2 * N_HEADS_DIL + h, bi, 0, 0)),
            pl.BlockSpec((None, 1, HEAD_DIM), lambda bi, h: (N_HEADS_SB + h, 0, 0)),
        ],
        out_specs=pl.BlockSpec((None, s, HEAD_DIM), lambda bi, h: (bi, 0, h)),
        scratch_shapes=[
            pltpu.VMEM((3, s, HEAD_DIM), F32),
            pltpu.VMEM((n_groups, s, HEAD_DIM), F32),
            pltpu.VMEM((n_groups, s, HEAD_DIM), F32),
        ],
        compiler_params=_params(("parallel", "parallel")),
        name="dil_attn",
    )(tables, qkv, qkv, qkv, gain_heads)


def _attn_out_kernel(x_ref, osb_ref, odl_ref, w_ref, gp_ref, out_ref):
    k_sb = osb_ref.shape[1]
    y = (jnp.dot(osb_ref[...], w_ref[:k_sb, :], preferred_element_type=F32)
         + jnp.dot(odl_ref[...].astype(BF16), w_ref[k_sb:, :], preferred_element_type=F32))
    out_ref[...] = x_ref[...] + _rms(y, gp_ref[...])


def _attn_out(x2, o_sb, o_dl, w, gain_post, *, tm):
    n, d = x2.shape
    row_spec = pl.BlockSpec((tm, d), lambda i: (i, 0))
    return pl.pallas_call(
        _attn_out_kernel,
        out_shape=jax.ShapeDtypeStruct((n, d), F32),
        grid=(n // tm,),
        in_specs=[
            row_spec,
            pl.BlockSpec((tm, o_sb.shape[1]), lambda i: (i, 0)),
            pl.BlockSpec((tm, o_dl.shape[1]), lambda i: (i, 0)),
            pl.BlockSpec(w.shape, lambda i: (0, 0), pipeline_mode=pl.Buffered(1)),
            pl.BlockSpec((1, d), lambda i: (0, 0)),
        ],
        out_specs=row_spec,
        compiler_params=_params(("parallel",)),
        name="attn_out",
    )(x2, o_sb, o_dl, w, gain_post)


def _ffn_kernel(x_ref, gpre_ref, wg_ref, wu_ref, wd_ref, gpost_ref, out_ref, h_ref, acc_ref):
    f = pl.program_id(1)

    @pl.when(f == 0)
    def _():
        h_ref[...] = _rms(x_ref[...], gpre_ref[...]).astype(BF16)
        acc_ref[...] = jnp.zeros_like(acc_ref)

    h = h_ref[...]
    g = jnp.dot(h, wg_ref[...], preferred_element_type=F32)
    u = jnp.dot(h, wu_ref[...], preferred_element_type=F32)
    act = (g * jax.nn.sigmoid(g) * u).astype(BF16)
    acc_ref[...] += jnp.dot(act, wd_ref[...], preferred_element_type=F32)

    @pl.when(f == pl.num_programs(1) - 1)
    def _():
        out_ref[...] = x_ref[...] + _rms(acc_ref[...], gpost_ref[...])


def _ffn(x2, gain_pre, w_gate_up, w_down, gain_post, *, tm, tf):
    n, d = x2.shape
    d_ff = w_down.shape[0]
    nf = d_ff // tf
    row_spec = pl.BlockSpec((tm, d), lambda i, f: (i, 0))
    vec_spec = pl.BlockSpec((1, d), lambda i, f: (0, 0))
    return pl.pallas_call(
        _ffn_kernel,
        out_shape=jax.ShapeDtypeStruct((n, d), F32),
        grid=(n // tm, nf),
        in_specs=[
            row_spec,
            vec_spec,
            pl.BlockSpec((d, tf), lambda i, f: (0, f)),
            pl.BlockSpec((d, tf), lambda i, f: (0, nf + f)),
            pl.BlockSpec((tf, d), lambda i, f: (f, 0)),
            vec_spec,
        ],
        out_specs=row_spec,
        scratch_shapes=[pltpu.VMEM((tm, d), BF16), pltpu.VMEM((tm, d), F32)],
        compiler_params=_params(("parallel", "arbitrary")),
        name="ffn",
    )(x2, gain_pre, w_gate_up, w_gate_up, w_down, gain_post)


def _pli_kernel(x_ref, p_ref, g_ref, wg_ref, wp_ref, out_ref):
    x = x_ref[...]
    h = _rms(x, g_ref[...]).astype(BF16)
    gate = jax.nn.sigmoid(jnp.dot(h, wg_ref[...], preferred_element_type=F32))
    emb = jnp.dot(p_ref[...].astype(BF16), wp_ref[...], preferred_element_type=F32)
    out_ref[...] = x + gate * emb


def _pli(x2, p3, layer, gain, w_gate, w_proj, *, tm):
    n, d = x2.shape
    dp = p3.shape[2]
    row_spec = pl.BlockSpec((tm, d), lambda i: (i, 0))
    return pl.pallas_call(
        _pli_kernel,
        out_shape=jax.ShapeDtypeStruct((n, d), F32),
        grid=(n // tm,),
        in_specs=[
            row_spec,
            pl.BlockSpec((None, tm, dp), lambda i: (layer, i, 0)),
            pl.BlockSpec((1, d), lambda i: (0, 0)),
            pl.BlockSpec(w_gate.shape, lambda i: (0, 0), pipeline_mode=pl.Buffered(1)),
            pl.BlockSpec(w_proj.shape, lambda i: (0, 0), pipeline_mode=pl.Buffered(1)),
        ],
        out_specs=row_spec,
        compiler_params=_params(("parallel",)),
        name="pli",
    )(x2, p3, gain, w_gate, w_proj)


class _Tiles(NamedTuple):
    qkv_rows: int
    sb_queries: int
    sb_keys: int
    row_block: int
    ffn_cols: int


def _tiles(n, s):
    return _Tiles(qkv_rows=min(1024, n), sb_queries=min(1024, s), sb_keys=256, row_block=min(512, n), ffn_cols=512)


def kernel(x, p, ln_mix_pre, w_in, ln_head, w_out, ln_mix_post, rel_bias, ln_ffn_pre, w_gate_up, w_down,
           ln_ffn_post, ln_pli, w_pli_gate, w_pli_proj):
    b, s, d = x.shape
    depth = w_in.shape[0]
    n = b * s
    assert d == N_HEADS * HEAD_DIM and w_in.shape[2] == 3 * d
    assert s % (BLOCK * DIL_GROUPS[-1][1]) == 0, "sequence must be a multiple of the widest dilated span"
    t = _tiles(n, s)

    tables = _bias_tables(rel_bias)
    x2 = x.reshape(n, d)
    p3 = p.reshape(depth, n, -1)
    for i in range(depth):
        gain_heads = ln_head[i].reshape(N_HEADS, 1, HEAD_DIM)
        qkv_sb, qkv_dl = _qkv_proj(x2, ln_mix_pre[i][None], w_in[i].astype(BF16), tm=t.qkv_rows)
        o_sb = _sb_attention(qkv_sb.reshape(-1, b, s, HEAD_DIM), gain_heads, b, s, tq=t.sb_queries, tk=t.sb_keys)
        o_dl = _dil_attention(qkv_dl.reshape(-1, b, s, HEAD_DIM), tables, gain_heads, b, s)
        x2 = _attn_out(x2, o_sb.reshape(n, -1), o_dl.reshape(n, -1), w_out[i].astype(BF16),
                       ln_mix_post[i][None], tm=t.row_block)
        x2 = _ffn(x2, ln_ffn_pre[i][None], w_gate_up[i].astype(BF16), w_down[i].astype(BF16),
                  ln_ffn_post[i][None], tm=t.row_block, tf=t.ffn_cols)
        x2 = _pli(x2, p3, i, ln_pli[i][None], w_pli_gate[i].astype(BF16), w_pli_proj[i].astype(BF16),
                  tm=t.row_block)
    return x2.reshape(b, s, d)
```

```python
import functools
import math
from typing import NamedTuple

import jax
import jax.numpy as jnp
import numpy as np
from jax import lax
from jax.experimental import pallas as pl
from jax.experimental.pallas import tpu as pltpu

HEAD_DIM = 128
N_HEADS_SB = 8
N_HEADS_DIL = 8
N_HEADS = N_HEADS_SB + N_HEADS_DIL
BLOCK = 128
DIL_GROUPS = ((128, 1), (512, 4), (2048, 16))
INTERLEAVE = 4
NUM_BUCKETS = 32
MAX_DISTANCE = 2048
RMS_EPS = 1e-6
MASK_VALUE = -1e30

VMEM_LIMIT_BYTES = 56 * 1024 * 1024

F32 = jnp.float32
BF16 = jnp.bfloat16
LOG2E = math.log2(math.e)
SATURATED = 151.0
_NT = (((1,), (1,)), ((), ()))


def _params(semantics):
    return pltpu.CompilerParams(dimension_semantics=semantics, vmem_limit_bytes=VMEM_LIMIT_BYTES)


def _rms(x, gain):
    ms = jnp.mean(x * x, axis=-1, keepdims=True)
    return x * lax.rsqrt(ms + RMS_EPS) * gain


def _qkv_kernel(x_ref, g_ref, w_ref, sb_ref, dl_ref, h_ref, *, n_sb_steps):
    j = pl.program_id(1)

    @pl.when(j == 0)
    def _():
        h_ref[...] = _rms(x_ref[...], g_ref[...]).astype(BF16)

    r = jnp.dot(h_ref[...], w_ref[...], preferred_element_type=F32)

    @pl.when(j < n_sb_steps)
    def _():
        for hh in range(sb_ref.shape[0]):
            sb_ref[hh] = r[:, hh * HEAD_DIM:(hh + 1) * HEAD_DIM].astype(BF16)

    @pl.when(j >= n_sb_steps)
    def _():
        for hh in range(dl_ref.shape[0]):
            dl_ref[hh] = r[:, hh * HEAD_DIM:(hh + 1) * HEAD_DIM]


def _qkv_proj(x2, gain, w, layer, *, tm):
    n, d = x2.shape
    heads_per_step = 8
    tn = heads_per_step * HEAD_DIM
    n_sb_steps = 3 * N_HEADS_SB // heads_per_step
    n_dl_steps = 3 * N_HEADS_DIL // heads_per_step
    assert w.shape[2] == (n_sb_steps + n_dl_steps) * tn
    return pl.pallas_call(
        functools.partial(_qkv_kernel, n_sb_steps=n_sb_steps),
        out_shape=[
            jax.ShapeDtypeStruct((3 * N_HEADS_SB, n, HEAD_DIM), BF16),
            jax.ShapeDtypeStruct((3 * N_HEADS_DIL, n, HEAD_DIM), F32),
        ],
        grid=(n // tm, n_sb_steps + n_dl_steps),
        in_specs=[
            pl.BlockSpec((tm, d), lambda i, j: (i, 0)),
            pl.BlockSpec((1, d), lambda i, j: (0, 0)),
            pl.BlockSpec((None, d, tn), lambda i, j: (layer, 0, j)),
        ],
        out_specs=[
            pl.BlockSpec((heads_per_step, tm, HEAD_DIM),
                         lambda i, j: (jnp.minimum(j, n_sb_steps - 1), i, 0)),
            pl.BlockSpec((heads_per_step, tm, HEAD_DIM),
                         lambda i, j: (jnp.maximum(j - n_sb_steps, 0), i, 0)),
        ],
        scratch_shapes=[pltpu.VMEM((tm, d), BF16)],
        compiler_params=_params(("parallel", "arbitrary")),
        name="qkv_proj",
    )(x2, gain, w)


def _sb_kernel(q_ref, k_ref, v_ref, g_ref, o_ref, u_ref, c_ref, acc_ref, *, tq, tk, scale):
    i = pl.program_id(2)
    nsub = tq // tk
    row_u = lax.broadcasted_iota(jnp.int32, (tk, tk), 0)
    col_u = lax.broadcasted_iota(jnp.int32, (tk, tk), 1)
    u_ref[...] = jnp.where(col_u >= row_u, 1.0, 0.0).astype(BF16)
    earlier = row_u < col_u

    def mask_first(x):
        first = jnp.where(earlier, x[:, :tk], 0.0)
        return first if x.shape[1] == tk else jnp.concatenate([first, x[:, tk:]], axis=1)

    def visit(tiles):
        staged = []
        for r0, kstart, diagonal in tiles:
            k = k_ref[pl.ds(kstart, tk), :]
            z2 = lax.dot_general(k, q_ref[r0:, :], _NT, preferred_element_type=F32) * (scale * LOG2E)
            neg_abs = pltpu.bitcast(pltpu.bitcast(z2, jnp.uint32) | jnp.uint32(0x80000000), F32)
            sp2 = jnp.maximum(z2, 0.0) + jnp.log2(1.0 + jnp.exp2(neg_abs))
            if diagonal:
                sp2 = mask_first(sp2)
            hi = sp2.astype(BF16)
            staged.append((z2, hi, (sp2 - hi.astype(F32)).astype(BF16)))
        u = u_ref[...]
        incls = [jnp.dot(u, hi, preferred_element_type=F32) + jnp.dot(u, lo, preferred_element_type=F32)
                 for _, hi, lo in staged]
        c = c_ref[...]
        before = []
        for (r0, _, _), incl in zip(tiles, incls):
            before.append(c[:, r0:])
            grown = c[:, r0:] + incl[0:1, :]
            c = grown if r0 == 0 else jnp.concatenate([c[:, :r0], grown], axis=1)
        c_ref[...] = c
        for (r0, kstart, diagonal), (z2, _, _), incl, c0 in zip(tiles, staged, incls, before):
            a = jnp.exp2(z2 - incl - c0)
            if diagonal:
                a = mask_first(a)
            v = v_ref[pl.ds(kstart, tk), :]
            acc_ref[:, r0:] += lax.dot_general(v, a.astype(BF16), (((0,), (0,)), ((), ())),
                                               preferred_element_type=F32)
        return c

    c_ref[...] = jnp.zeros_like(c_ref)
    acc_ref[...] = jnp.zeros_like(acc_ref)

    diag = [(jj * tk, pl.multiple_of(i * tq + jj * tk, tk), True) for jj in reversed(range(nsub))]
    n = i * nsub

    def left(s):
        return (0, pl.multiple_of(jnp.maximum(n - 1 - s, 0) * tk, tk), False)

    def saturated(c):
        return (jnp.min(c) >= SATURATED).astype(jnp.int32)

    done = lax.cond(n > 0, lambda: saturated(visit(diag + [left(0)])), lambda: saturated(visit(diag)))

    def step(state):
        s, _ = state
        return s + 1, saturated(visit([left(s)]))

    lax.while_loop(lambda state: (state[0] < n) & (state[1] == 0), step, (jnp.int32(1), done))
    o_ref[...] = _rms(acc_ref[...].T, g_ref[...]).astype(BF16)


def _sb_attention(qkv, gain_heads, b, s, *, tq, tk):
    kern = functools.partial(_sb_kernel, tq=tq, tk=tk, scale=1.0 / math.sqrt(HEAD_DIM))
    return pl.pallas_call(
        kern,
        out_shape=jax.ShapeDtypeStruct((b, s, N_HEADS_SB * HEAD_DIM), BF16),
        grid=(b, N_HEADS_SB, s // tq),
        in_specs=[
            pl.BlockSpec((None, None, tq, HEAD_DIM), lambda bi, h, i: (h, bi, i, 0)),
            pl.BlockSpec((None, None, s, HEAD_DIM), lambda bi, h, i: (N_HEADS_SB + h, bi, 0, 0)),
            pl.BlockSpec((None, None, s, HEAD_DIM), lambda bi, h, i: (2 * N_HEADS_SB + h, bi, 0, 0)),
            pl.BlockSpec((None, 1, HEAD_DIM), lambda bi, h, i: (h, 0, 0)),
        ],
        out_specs=pl.BlockSpec((None, tq, HEAD_DIM), lambda bi, h, i: (bi, i, h)),
        scratch_shapes=[
            pltpu.VMEM((tk, tk), BF16),
            pltpu.VMEM((1, tq), F32),
            pltpu.VMEM((HEAD_DIM, tq), F32),
        ],
        compiler_params=_params(("parallel", "parallel", "arbitrary")),
        name="sb_attn",
    )(qkv, qkv, qkv, gain_heads)


def _t5_bucket_np(dist):
    max_exact = NUM_BUCKETS // 2
    d = np.maximum(dist, 1).astype(np.float64)
    large = max_exact + (np.log(d / max_exact) / math.log(MAX_DISTANCE / max_exact)
                         * (NUM_BUCKETS - max_exact)).astype(np.int64)
    large = np.minimum(large, NUM_BUCKETS - 1)
    return np.where(dist < max_exact, dist, large)


def _bias_tables(rel_bias):
    qi = np.arange(BLOCK)[:, None]
    ki = np.arange(2 * BLOCK)[None, :]
    rel = BLOCK + qi - ki
    band = (rel >= 0) & (rel <= BLOCK)
    bias_by_bucket = rel_bias.astype(F32)
    tables = []
    for _, dilation in DIL_GROUPS:
        bucket = _t5_bucket_np(np.maximum(rel, 0) * dilation)
        table = jnp.full((bias_by_bucket.shape[1], BLOCK, 2 * BLOCK), MASK_VALUE, F32)
        for bkt in np.unique(bucket[band]):
            table = jnp.where((band & (bucket == bkt))[None], bias_by_bucket[bkt][:, None, None], table)
        tables.append(table)
    return jnp.stack(tables, axis=0)


def _dil_kernel(tbl_ref, q_ref, k_ref, v_ref, g_ref, o_ref, t_ref, og_ref, lg_ref, *, seq, scale, tiles_per_step):
    quarter = seq // INTERLEAVE
    for x, ref in enumerate((q_ref, k_ref, v_ref)):
        for a in range(INTERLEAVE):
            t_ref[x, a * quarter:(a + 1) * quarter, :] = ref[pl.ds(a, quarter, stride=INTERLEAVE), :]

    def rows(start, count, stride):
        if stride == 1:
            return pl.ds(start if isinstance(start, int) else pl.multiple_of(start, BLOCK), count)
        return pl.ds(start, count, stride=stride)

    sources = (
        (lambda x, rws: (q_ref, k_ref, v_ref)[x][rws, :], 1, 1, lambda r: 0),
        (lambda x, rws: t_ref[x, rws, :], 1, INTERLEAVE, lambda r: r * quarter),
        (lambda x, rws: t_ref[x, rws, :], INTERLEAVE, INTERLEAVE * INTERLEAVE,
         lambda r: (r % INTERLEAVE) * quarter + r // INTERLEAVE),
    )

    def emit(g, stride, tiles):
        logits = [lax.dot_general(q, k2, _NT, preferred_element_type=F32) * scale + tbl
                  for q, k2, _, tbl, _ in tiles]
        stats = []
        for s in logits:
            m = jnp.max(s, axis=-1, keepdims=True)
            p = jnp.exp(s - m)
            stats.append((m, p.astype(BF16), jnp.sum(p, axis=-1, keepdims=True)))
        for (m, p, denom), (_, _, v2, _, start) in zip(stats, tiles):
            o = jnp.dot(p, v2, preferred_element_type=F32) / denom
            og_ref[g, rows(start, BLOCK, stride), :] = o
            lg_ref[g, rows(start, BLOCK, stride), :] = jnp.broadcast_to(m + jnp.log(denom), (BLOCK, HEAD_DIM))

    for g, (_, d) in enumerate(DIL_GROUPS):
        src, stride, n_sub, base_of = sources[g]
        assert n_sub == d
        nb = seq // (BLOCK * d)
        nblk = min(tiles_per_step, nb)
        span = BLOCK * stride

        def load(x, start, count, src=src, stride=stride):
            return src(x, rows(start, count, stride)).astype(BF16)

        def first(base, g=g, nblk=nblk, span=span, load=load):
            q = load(0, base, nblk * BLOCK)
            k = load(1, base, nblk * BLOCK)
            v = load(2, base, nblk * BLOCK)
            tiles = [(q[:BLOCK], k[:BLOCK], v[:BLOCK], tbl_ref[g, :, BLOCK:], base)]
            for m in range(1, nblk):
                tiles.append((q[m * BLOCK:(m + 1) * BLOCK], k[(m - 1) * BLOCK:(m + 1) * BLOCK],
                              v[(m - 1) * BLOCK:(m + 1) * BLOCK], tbl_ref[g], base + m * span))
            return tiles

        def later(base, n0, g=g, nblk=nblk, span=span, load=load):
            q = load(0, base + n0 * span, nblk * BLOCK)
            k = load(1, base + (n0 - 1) * span, (nblk + 1) * BLOCK)
            v = load(2, base + (n0 - 1) * span, (nblk + 1) * BLOCK)
            return [(q[m * BLOCK:(m + 1) * BLOCK], k[m * BLOCK:(m + 2) * BLOCK],
                     v[m * BLOCK:(m + 2) * BLOCK], tbl_ref[g], base + (n0 + m) * span) for m in range(nblk)]

        unroll = max(1, min(d, tiles_per_step // nb))

        def r_body(t, carry, g=g, stride=stride, nb=nb, nblk=nblk, unroll=unroll, first=first, later=later,
                   base_of=base_of):
            bases = [base_of(t * unroll + j) for j in range(unroll)]
            emit(g, stride, [tl for base in bases for tl in first(base)])
            if nb > nblk:
                def body(u, carry2):
                    emit(g, stride, [tl for base in bases for tl in later(base, u * nblk)])
                    return carry2
                lax.fori_loop(1, nb // nblk, body, 0)
            return carry

        if d == unroll:
            r_body(0, 0)
        else:
            lax.fori_loop(0, d // unroll, r_body, 0)

    gain = g_ref[...]

    def finish(a, carry):
        seq_rows = pl.ds(a, quarter, stride=INTERLEAVE)
        t_rows = pl.ds(pl.multiple_of(a * quarter, BLOCK), quarter)
        l0, l1, l2 = lg_ref[0, seq_rows, :], lg_ref[1, t_rows, :], lg_ref[2, t_rows, :]
        m = jnp.maximum(jnp.maximum(l0, l1), l2)
        e0, e1, e2 = jnp.exp(l0 - m), jnp.exp(l1 - m), jnp.exp(l2 - m)
        o = ((e0 * og_ref[0, seq_rows, :] + e1 * og_ref[1, t_rows, :] + e2 * og_ref[2, t_rows, :])
             / (e0 + e1 + e2))
        o_ref[seq_rows, :] = _rms(o, gain)
        return carry

    lax.fori_loop(0, INTERLEAVE, finish, 0)


def _dil_attention(qkv, tables, gain_heads, b, s):
    n_groups = len(DIL_GROUPS)
    assert tuple(d for _, d in DIL_GROUPS) == (1, INTERLEAVE, INTERLEAVE * INTERLEAVE)
    kern = functools.partial(_dil_kernel, seq=s, scale=1.0 / math.sqrt(HEAD_DIM), tiles_per_step=16)
    return pl.pallas_call(
        kern,
        out_shape=jax.ShapeDtypeStruct((b, s, N_HEADS_DIL * HEAD_DIM), F32),
        grid=(b, N_HEADS_DIL),
        in_specs=[
            pl.BlockSpec((n_groups, None, BLOCK, 2 * BLOCK), lambda bi, h: (0, h, 0, 0)),
            pl.BlockSpec((None, None, s, HEAD_DIM), lambda bi, h: (h, bi, 0, 0)),
            pl.BlockSpec((None, None, s, HEAD_DIM), lambda bi, h: (N_HEADS_DIL + h, bi, 0, 0)),
            pl.BlockSpec((None, None, s, HEAD_DIM), lambda bi, h: (2 * N_HEADS_DIL + h, bi, 0, 0)),
            pl.BlockSpec((None, 1, HEAD_DIM), lambda bi, h: (N_HEADS_SB + h, 0, 0)),
        ],
        out_specs=pl.BlockSpec((None, s, HEAD_DIM), lambda bi, h: (bi, 0, h)),
        scratch_shapes=[
            pltpu.VMEM((3, s, HEAD_DIM), F32),
            pltpu.VMEM((n_groups, s, HEAD_DIM), F32),
            pltpu.VMEM((n_groups, s, HEAD_DIM), F32),
        ],
        compiler_params=_params(("parallel", "parallel")),
        name="dil_attn",
    )(tables, qkv, qkv, qkv, gain_heads)


def _attn_out_kernel(x_ref, osb_ref, odl_ref, w_ref, gp_ref, out_ref):
    k_sb = osb_ref.shape[1]
    y = (jnp.dot(osb_ref[...], w_ref[:k_sb, :], preferred_element_type=F32)
         + jnp.dot(odl_ref[...].astype(BF16), w_ref[k_sb:, :], preferred_element_type=F32))
    out_ref[...] = x_ref[...] + _rms(y, gp_ref[...])


def _attn_out(x2, o_sb, o_dl, w, layer, gain_post, *, tm):
    n, d = x2.shape
    row_spec = pl.BlockSpec((tm, d), lambda i: (i, 0))
    return pl.pallas_call(
        _attn_out_kernel,
        out_shape=jax.ShapeDtypeStruct((n, d), F32),
        grid=(n // tm,),
        in_specs=[
            row_spec,
            pl.BlockSpec((tm, o_sb.shape[1]), lambda i: (i, 0)),
            pl.BlockSpec((tm, o_dl.shape[1]), lambda i: (i, 0)),
            pl.BlockSpec((None,) + w.shape[1:], lambda i: (layer, 0, 0), pipeline_mode=pl.Buffered(1)),
            pl.BlockSpec((1, d), lambda i: (0, 0)),
        ],
        out_specs=row_spec,
        compiler_params=_params(("parallel",)),
        name="attn_out",
    )(x2, o_sb, o_dl, w, gain_post)


def _ffn_kernel(x_ref, gpre_ref, wg_ref, wu_ref, wd_ref, gpost_ref, out_ref, h_ref, acc_ref):
    f = pl.program_id(1)

    @pl.when(f == 0)
    def _():
        h_ref[...] = _rms(x_ref[...], gpre_ref[...]).astype(BF16)
        acc_ref[...] = jnp.zeros_like(acc_ref)

    h = h_ref[...]
    g = jnp.dot(h, wg_ref[...], preferred_element_type=F32)
    u = jnp.dot(h, wu_ref[...], preferred_element_type=F32)
    act = (g * jax.nn.sigmoid(g) * u).astype(BF16)
    acc_ref[...] += jnp.dot(act, wd_ref[...], preferred_element_type=F32)

    @pl.when(f == pl.num_programs(1) - 1)
    def _():
        out_ref[...] = x_ref[...] + _rms(acc_ref[...], gpost_ref[...])


def _ffn(x2, gain_pre, w_gate_up, w_down, layer, gain_post, *, tm, tf):
    n, d = x2.shape
    d_ff = w_down.shape[1]
    nf = d_ff // tf
    row_spec = pl.BlockSpec((tm, d), lambda i, f: (i, 0))
    vec_spec = pl.BlockSpec((1, d), lambda i, f: (0, 0))
    return pl.pallas_call(
        _ffn_kernel,
        out_shape=jax.ShapeDtypeStruct((n, d), F32),
        grid=(n // tm, nf),
        in_specs=[
            row_spec,
            vec_spec,
            pl.BlockSpec((None, d, tf), lambda i, f: (layer, 0, f)),
            pl.BlockSpec((None, d, tf), lambda i, f: (layer, 0, nf + f)),
            pl.BlockSpec((None, tf, d), lambda i, f: (layer, f, 0)),
            vec_spec,
        ],
        out_specs=row_spec,
        scratch_shapes=[pltpu.VMEM((tm, d), BF16), pltpu.VMEM((tm, d), F32)],
        compiler_params=_params(("parallel", "arbitrary")),
        name="ffn",
    )(x2, gain_pre, w_gate_up, w_gate_up, w_down, gain_post)


def _pli_kernel(x_ref, p_ref, g_ref, wg_ref, wp_ref, out_ref):
    x = x_ref[...]
    h = _rms(x, g_ref[...]).astype(BF16)
    gate = jax.nn.sigmoid(jnp.dot(h, wg_ref[...], preferred_element_type=F32))
    emb = jnp.dot(p_ref[...].astype(BF16), wp_ref[...], preferred_element_type=F32)
    out_ref[...] = x + gate * emb


def _pli(x2, p3, layer, gain, w_gate, w_proj, *, tm):
    n, d = x2.shape
    dp = p3.shape[2]
    row_spec = pl.BlockSpec((tm, d), lambda i: (i, 0))
    return pl.pallas_call(
        _pli_kernel,
        out_shape=jax.ShapeDtypeStruct((n, d), F32),
        grid=(n // tm,),
        in_specs=[
            row_spec,
            pl.BlockSpec((None, tm, dp), lambda i: (layer, i, 0)),
            pl.BlockSpec((1, d), lambda i: (0, 0)),
            pl.BlockSpec((None,) + w_gate.shape[1:], lambda i: (layer, 0, 0), pipeline_mode=pl.Buffered(1)),
            pl.BlockSpec((None,) + w_proj.shape[1:], lambda i: (layer, 0, 0), pipeline_mode=pl.Buffered(1)),
        ],
        out_specs=row_spec,
        compiler_params=_params(("parallel",)),
        name="pli",
    )(x2, p3, gain, w_gate, w_proj)


class _Tiles(NamedTuple):
    qkv_rows: int
    sb_queries: int
    sb_keys: int
    row_block: int
    ffn_cols: int


def _tiles(n, s):
    return _Tiles(qkv_rows=min(1024, n), sb_queries=min(1024, s), sb_keys=256, row_block=min(512, n), ffn_cols=512)


def kernel(x, p, ln_mix_pre, w_in, ln_head, w_out, ln_mix_post, rel_bias, ln_ffn_pre, w_gate_up, w_down,
           ln_ffn_post, ln_pli, w_pli_gate, w_pli_proj):
    b, s, d = x.shape
    depth = w_in.shape[0]
    n = b * s
    assert d == N_HEADS * HEAD_DIM and w_in.shape[2] == 3 * d
    assert s % (BLOCK * DIL_GROUPS[-1][1]) == 0, "sequence must be a multiple of the widest dilated span"
    t = _tiles(n, s)

    tables = _bias_tables(rel_bias)
    x2 = x.reshape(n, d)
    p3 = p.reshape(depth, n, -1)
    w_in, w_out, w_gate_up, w_down, w_pli_gate, w_pli_proj = (
        w.astype(BF16) for w in (w_in, w_out, w_gate_up, w_down, w_pli_gate, w_pli_proj))
    for i in range(depth):
        gain_heads = ln_head[i].reshape(N_HEADS, 1, HEAD_DIM)
        qkv_sb, qkv_dl = _qkv_proj(x2, ln_mix_pre[i][None], w_in, i, tm=t.qkv_rows)
        o_sb = _sb_attention(qkv_sb.reshape(-1, b, s, HEAD_DIM), gain_heads, b, s, tq=t.sb_queries, tk=t.sb_keys)
        o_dl = _dil_attention(qkv_dl.reshape(-1, b, s, HEAD_DIM), tables, gain_heads, b, s)
        x2 = _attn_out(x2, o_sb.reshape(n, -1), o_dl.reshape(n, -1), w_out, i, ln_mix_post[i][None],
                       tm=t.row_block)
        x2 = _ffn(x2, ln_ffn_pre[i][None], w_gate_up, w_down, i, ln_ffn_post[i][None],
                  tm=t.row_block, tf=t.ffn_cols)
        x2 = _pli(x2, p3, i, ln_pli[i][None], w_pli_gate, w_pli_proj, tm=t.row_block)
    return x2.reshape(b, s, d)
```

```python
import functools
import math
from typing import NamedTuple

import jax
import jax.numpy as jnp
import numpy as np
from jax import lax
from jax.experimental import pallas as pl
from jax.experimental.pallas import tpu as pltpu

HEAD_DIM = 128
N_HEADS_SB = 8
N_HEADS_DIL = 8
N_HEADS = N_HEADS_SB + N_HEADS_DIL
BLOCK = 128
DIL_GROUPS = ((128, 1), (512, 4), (2048, 16))
INTERLEAVE = 4
NUM_BUCKETS = 32
MAX_DISTANCE = 2048
RMS_EPS = 1e-6
MASK_VALUE = -1e30

VMEM_LIMIT_BYTES = 56 * 1024 * 1024

F32 = jnp.float32
BF16 = jnp.bfloat16
LOG2E = math.log2(math.e)
SATURATED = 151.0
_NT = (((1,), (1,)), ((), ()))


def _params(semantics):
    return pltpu.CompilerParams(dimension_semantics=semantics, vmem_limit_bytes=VMEM_LIMIT_BYTES)


def _rms(x, gain):
    ms = jnp.mean(x * x, axis=-1, keepdims=True)
    return x * lax.rsqrt(ms + RMS_EPS) * gain


def _qkv_kernel(x_ref, g_ref, w_ref, sb_ref, dl_ref, h_ref, *, n_sb_steps):
    j = pl.program_id(1)

    @pl.when(j == 0)
    def _():
        h_ref[...] = _rms(x_ref[...], g_ref[...]).astype(BF16)

    r = jnp.dot(h_ref[...], w_ref[...], preferred_element_type=F32)

    @pl.when(j < n_sb_steps)
    def _():
        for hh in range(sb_ref.shape[0]):
            sb_ref[hh] = r[:, hh * HEAD_DIM:(hh + 1) * HEAD_DIM].astype(BF16)

    @pl.when(j >= n_sb_steps)
    def _():
        for hh in range(dl_ref.shape[0]):
            dl_ref[hh] = r[:, hh * HEAD_DIM:(hh + 1) * HEAD_DIM]


def _qkv_proj(x2, gain, w, layer, *, tm):
    n, d = x2.shape
    heads_per_step = 8
    tn = heads_per_step * HEAD_DIM
    n_sb_steps = 3 * N_HEADS_SB // heads_per_step
    n_dl_steps = 3 * N_HEADS_DIL // heads_per_step
    assert w.shape[2] == (n_sb_steps + n_dl_steps) * tn
    return pl.pallas_call(
        functools.partial(_qkv_kernel, n_sb_steps=n_sb_steps),
        out_shape=[
            jax.ShapeDtypeStruct((3 * N_HEADS_SB, n, HEAD_DIM), BF16),
            jax.ShapeDtypeStruct((3 * N_HEADS_DIL, n, HEAD_DIM), F32),
        ],
        grid=(n // tm, n_sb_steps + n_dl_steps),
        in_specs=[
            pl.BlockSpec((tm, d), lambda i, j: (i, 0)),
            pl.BlockSpec((1, d), lambda i, j: (0, 0)),
            pl.BlockSpec((None, d, tn), lambda i, j: (layer, 0, j)),
        ],
        out_specs=[
            pl.BlockSpec((heads_per_step, tm, HEAD_DIM),
                         lambda i, j: (jnp.minimum(j, n_sb_steps - 1), i, 0)),
            pl.BlockSpec((heads_per_step, tm, HEAD_DIM),
                         lambda i, j: (jnp.maximum(j - n_sb_steps, 0), i, 0)),
        ],
        scratch_shapes=[pltpu.VMEM((tm, d), BF16)],
        compiler_params=_params(("parallel", "arbitrary")),
        name="qkv_proj",
    )(x2, gain, w)


def _sb_kernel(q_ref, k_ref, v_ref, g_ref, o_ref, u_ref, c_ref, acc_ref, *, tq, tk, scale):
    i = pl.program_id(2)
    nsub = tq // tk
    row_u = lax.broadcasted_iota(jnp.int32, (tk, tk), 0)
    col_u = lax.broadcasted_iota(jnp.int32, (tk, tk), 1)
    u_ref[...] = jnp.where(col_u >= row_u, 1.0, 0.0).astype(BF16)
    earlier = row_u < col_u

    def mask_first(x):
        first = jnp.where(earlier, x[:, :tk], 0.0)
        return first if x.shape[1] == tk else jnp.concatenate([first, x[:, tk:]], axis=1)

    def visit(tiles):
        staged = []
        for r0, kstart, diagonal in tiles:
            k = k_ref[pl.ds(kstart, tk), :]
            z2 = lax.dot_general(k, q_ref[r0:, :], _NT, preferred_element_type=F32) * (scale * LOG2E)
            neg_abs = pltpu.bitcast(pltpu.bitcast(z2, jnp.uint32) | jnp.uint32(0x80000000), F32)
            sp2 = jnp.maximum(z2, 0.0) + jnp.log2(1.0 + jnp.exp2(neg_abs))
            if diagonal:
                sp2 = mask_first(sp2)
            hi = sp2.astype(BF16)
            staged.append((z2, hi, (sp2 - hi.astype(F32)).astype(BF16)))
        u = u_ref[...]
        incls = [jnp.dot(u, hi, preferred_element_type=F32) + jnp.dot(u, lo, preferred_element_type=F32)
                 for _, hi, lo in staged]
        c = c_ref[...]
        before = []
        for (r0, _, _), incl in zip(tiles, incls):
            before.append(c[:, r0:])
            grown = c[:, r0:] + incl[0:1, :]
            c = grown if r0 == 0 else jnp.concatenate([c[:, :r0], grown], axis=1)
        c_ref[...] = c
        for (r0, kstart, diagonal), (z2, _, _), incl, c0 in zip(tiles, staged, incls, before):
            a = jnp.exp2(z2 - incl - c0)
            if diagonal:
                a = mask_first(a)
            v = v_ref[pl.ds(kstart, tk), :]
            acc_ref[:, r0:] += lax.dot_general(v, a.astype(BF16), (((0,), (0,)), ((), ())),
                                               preferred_element_type=F32)
        return c

    c_ref[...] = jnp.zeros_like(c_ref)
    acc_ref[...] = jnp.zeros_like(acc_ref)

    diag = [(jj * tk, pl.multiple_of(i * tq + jj * tk, tk), True) for jj in reversed(range(nsub))]
    n = i * nsub

    def left(s):
        return (0, pl.multiple_of(jnp.maximum(n - 1 - s, 0) * tk, tk), False)

    def saturated(c):
        return (jnp.min(c) >= SATURATED).astype(jnp.int32)

    done = lax.cond(n > 0, lambda: saturated(visit(diag + [left(0)])), lambda: saturated(visit(diag)))

    def step(state):
        s, _ = state
        return s + 1, saturated(visit([left(s)]))

    lax.while_loop(lambda state: (state[0] < n) & (state[1] == 0), step, (jnp.int32(1), done))
    o_ref[...] = _rms(acc_ref[...].T, g_ref[...]).astype(BF16)


def _sb_attention(qkv, gain_heads, b, s, *, tq, tk):
    kern = functools.partial(_sb_kernel, tq=tq, tk=tk, scale=1.0 / math.sqrt(HEAD_DIM))
    return pl.pallas_call(
        kern,
        out_shape=jax.ShapeDtypeStruct((b, s, N_HEADS_SB * HEAD_DIM), BF16),
        grid=(b, N_HEADS_SB, s // tq),
        in_specs=[
            pl.BlockSpec((None, None, tq, HEAD_DIM), lambda bi, h, i: (h, bi, i, 0)),
            pl.BlockSpec((None, None, s, HEAD_DIM), lambda bi, h, i: (N_HEADS_SB + h, bi, 0, 0)),
            pl.BlockSpec((None, None, s, HEAD_DIM), lambda bi, h, i: (2 * N_HEADS_SB + h, bi, 0, 0)),
            pl.BlockSpec((None, 1, HEAD_DIM), lambda bi, h, i: (h, 0, 0)),
        ],
        out_specs=pl.BlockSpec((None, tq, HEAD_DIM), lambda bi, h, i: (bi, i, h)),
        scratch_shapes=[
            pltpu.VMEM((tk, tk), BF16),
            pltpu.VMEM((1, tq), F32),
            pltpu.VMEM((HEAD_DIM, tq), F32),
        ],
        compiler_params=_params(("parallel", "parallel", "arbitrary")),
        name="sb_attn",
    )(qkv, qkv, qkv, gain_heads)


def _t5_bucket_np(dist):
    max_exact = NUM_BUCKETS // 2
    d = np.maximum(dist, 1).astype(np.float64)
    large = max_exact + (np.log(d / max_exact) / math.log(MAX_DISTANCE / max_exact)
                         * (NUM_BUCKETS - max_exact)).astype(np.int64)
    large = np.minimum(large, NUM_BUCKETS - 1)
    return np.where(dist < max_exact, dist, large)


def _bias_tables(rel_bias):
    qi = np.arange(BLOCK)[:, None]
    ki = np.arange(2 * BLOCK)[None, :]
    rel = BLOCK + qi - ki
    band = (rel >= 0) & (rel <= BLOCK)
    bias_by_bucket = rel_bias.astype(F32)
    tables = []
    for _, dilation in DIL_GROUPS:
        bucket = _t5_bucket_np(np.maximum(rel, 0) * dilation)
        table = jnp.full((bias_by_bucket.shape[1], BLOCK, 2 * BLOCK), MASK_VALUE, F32)
        for bkt in np.unique(bucket[band]):
            table = jnp.where((band & (bucket == bkt))[None], bias_by_bucket[bkt][:, None, None], table)
        tables.append(table)
    return jnp.stack(tables, axis=0) * LOG2E


def _dil_kernel(tbl_ref, q_ref, k_ref, v_ref, g_ref, o_ref, t_ref, og_ref, lg_ref, *, seq, scale, tiles_per_step):
    quarter = seq // INTERLEAVE
    for x, ref in enumerate((q_ref, k_ref, v_ref)):
        for a in range(INTERLEAVE):
            t_ref[x, a * quarter:(a + 1) * quarter, :] = ref[pl.ds(a, quarter, stride=INTERLEAVE), :]

    def rows(start, count, stride):
        if stride == 1:
            return pl.ds(start if isinstance(start, int) else pl.multiple_of(start, BLOCK), count)
        return pl.ds(start, count, stride=stride)

    sources = (
        (lambda x, rws: (q_ref, k_ref, v_ref)[x][rws, :], 1, 1, lambda r: 0),
        (lambda x, rws: t_ref[x, rws, :], 1, INTERLEAVE, lambda r: r * quarter),
        (lambda x, rws: t_ref[x, rws, :], INTERLEAVE, INTERLEAVE * INTERLEAVE,
         lambda r: (r % INTERLEAVE) * quarter + r // INTERLEAVE),
    )

    def emit(g, stride, tiles):
        logits = [lax.dot_general(q, k2, _NT, preferred_element_type=F32) * (scale * LOG2E) + tbl
                  for q, k2, _, tbl, _ in tiles]
        stats = []
        for s in logits:
            m = jnp.max(s, axis=-1, keepdims=True)
            p = jnp.exp2(s - m)
            stats.append((m, p.astype(BF16), jnp.sum(p, axis=-1, keepdims=True)))
        for (m, p, denom), (_, _, v2, _, start) in zip(stats, tiles):
            o = jnp.dot(p, v2, preferred_element_type=F32) / denom
            og_ref[g, rows(start, BLOCK, stride), :] = o
            lg_ref[g, rows(start, BLOCK, stride), :] = jnp.broadcast_to(m + jnp.log2(denom), (BLOCK, HEAD_DIM))

    for g, (_, d) in enumerate(DIL_GROUPS):
        src, stride, n_sub, base_of = sources[g]
        assert n_sub == d
        nb = seq // (BLOCK * d)
        nblk = min(tiles_per_step, nb)
        span = BLOCK * stride

        def load(x, start, count, src=src, stride=stride):
            return src(x, rows(start, count, stride)).astype(BF16)

        def first(base, g=g, nblk=nblk, span=span, load=load):
            q = load(0, base, nblk * BLOCK)
            k = load(1, base, nblk * BLOCK)
            v = load(2, base, nblk * BLOCK)
            tiles = [(q[:BLOCK], k[:BLOCK], v[:BLOCK], tbl_ref[g, :, BLOCK:], base)]
            for m in range(1, nblk):
                tiles.append((q[m * BLOCK:(m + 1) * BLOCK], k[(m - 1) * BLOCK:(m + 1) * BLOCK],
                              v[(m - 1) * BLOCK:(m + 1) * BLOCK], tbl_ref[g], base + m * span))
            return tiles

        def later(base, n0, g=g, nblk=nblk, span=span, load=load):
            q = load(0, base + n0 * span, nblk * BLOCK)
            k = load(1, base + (n0 - 1) * span, (nblk + 1) * BLOCK)
            v = load(2, base + (n0 - 1) * span, (nblk + 1) * BLOCK)
            return [(q[m * BLOCK:(m + 1) * BLOCK], k[m * BLOCK:(m + 2) * BLOCK],
                     v[m * BLOCK:(m + 2) * BLOCK], tbl_ref[g], base + (n0 + m) * span) for m in range(nblk)]

        unroll = max(1, min(d, tiles_per_step // nb))

        def r_body(t, carry, g=g, stride=stride, nb=nb, nblk=nblk, unroll=unroll, first=first, later=later,
                   base_of=base_of):
            bases = [base_of(t * unroll + j) for j in range(unroll)]
            emit(g, stride, [tl for base in bases for tl in first(base)])
            if nb > nblk:
                def body(u, carry2):
                    emit(g, stride, [tl for base in bases for tl in later(base, u * nblk)])
                    return carry2
                lax.fori_loop(1, nb // nblk, body, 0)
            return carry

        if d == unroll:
            r_body(0, 0)
        else:
            lax.fori_loop(0, d // unroll, r_body, 0)

    gain = g_ref[...]

    def finish(a, carry):
        seq_rows = pl.ds(a, quarter, stride=INTERLEAVE)
        t_rows = pl.ds(pl.multiple_of(a * quarter, BLOCK), quarter)
        l0, l1, l2 = lg_ref[0, seq_rows, :], lg_ref[1, t_rows, :], lg_ref[2, t_rows, :]
        m = jnp.maximum(jnp.maximum(l0, l1), l2)
        e0, e1, e2 = jnp.exp2(l0 - m), jnp.exp2(l1 - m), jnp.exp2(l2 - m)
        o = ((e0 * og_ref[0, seq_rows, :] + e1 * og_ref[1, t_rows, :] + e2 * og_ref[2, t_rows, :])
             / (e0 + e1 + e2))
        o_ref[seq_rows, :] = _rms(o, gain)
        return carry

    lax.fori_loop(0, INTERLEAVE, finish, 0)


def _dil_attention(qkv, tables, gain_heads, b, s):
    n_groups = len(DIL_GROUPS)
    assert tuple(d for _, d in DIL_GROUPS) == (1, INTERLEAVE, INTERLEAVE * INTERLEAVE)
    kern = functools.partial(_dil_kernel, seq=s, scale=1.0 / math.sqrt(HEAD_DIM), tiles_per_step=16)
    return pl.pallas_call(
        kern,
        out_shape=jax.ShapeDtypeStruct((b, s, N_HEADS_DIL * HEAD_DIM), F32),
        grid=(b, N_HEADS_DIL),
        in_specs=[
            pl.BlockSpec((n_groups, None, BLOCK, 2 * BLOCK), lambda bi, h: (0, h, 0, 0)),
            pl.BlockSpec((None, None, s, HEAD_DIM), lambda bi, h: (h, bi, 0, 0)),
            pl.BlockSpec((None, None, s, HEAD_DIM), lambda bi, h: (N_HEADS_DIL + h, bi, 0, 0)),
            pl.BlockSpec((None, None, s, HEAD_DIM), lambda bi, h: (2 * N_HEADS_DIL + h, bi, 0, 0)),
            pl.BlockSpec((None, 1, HEAD_DIM), lambda bi, h: (N_HEADS_SB + h, 0, 0)),
        ],
        out_specs=pl.BlockSpec((None, s, HEAD_DIM), lambda bi, h: (bi, 0, h)),
        scratch_shapes=[
            pltpu.VMEM((3, s, HEAD_DIM), F32),
            pltpu.VMEM((n_groups, s, HEAD_DIM), F32),
            pltpu.VMEM((n_groups, s, HEAD_DIM), F32),
        ],
        compiler_params=_params(("parallel", "parallel")),
        name="dil_attn",
    )(tables, qkv, qkv, qkv, gain_heads)


def _attn_out_kernel(x_ref, osb_ref, odl_ref, w_ref, gp_ref, out_ref):
    k_sb = osb_ref.shape[1]
    y = (jnp.dot(osb_ref[...], w_ref[:k_sb, :], preferred_element_type=F32)
         + jnp.dot(odl_ref[...].astype(BF16), w_ref[k_sb:, :], preferred_element_type=F32))
    out_ref[...] = x_ref[...] + _rms(y, gp_ref[...])


def _attn_out(x2, o_sb, o_dl, w, layer, gain_post, *, tm):
    n, d = x2.shape
    row_spec = pl.BlockSpec((tm, d), lambda i: (i, 0))
    return pl.pallas_call(
        _attn_out_kernel,
        out_shape=jax.ShapeDtypeStruct((n, d), F32),
        grid=(n // tm,),
        in_specs=[
            row_spec,
            pl.BlockSpec((tm, o_sb.shape[1]), lambda i: (i, 0)),
            pl.BlockSpec((tm, o_dl.shape[1]), lambda i: (i, 0)),
            pl.BlockSpec((None,) + w.shape[1:], lambda i: (layer, 0, 0), pipeline_mode=pl.Buffered(1)),
            pl.BlockSpec((1, d), lambda i: (0, 0)),
        ],
        out_specs=row_spec,
        compiler_params=_params(("parallel",)),
        name="attn_out",
    )(x2, o_sb, o_dl, w, gain_post)


def _ffn_kernel(x_ref, gpre_ref, wg_ref, wu_ref, wd_ref, gpost_ref, out_ref, h_ref, acc_ref):
    f = pl.program_id(1)

    @pl.when(f == 0)
    def _():
        h_ref[...] = _rms(x_ref[...], gpre_ref[...]).astype(BF16)
        acc_ref[...] = jnp.zeros_like(acc_ref)

    h = h_ref[...]
    g = jnp.dot(h, wg_ref[...], preferred_element_type=F32)
    u = jnp.dot(h, wu_ref[...], preferred_element_type=F32)
    act = (g * jax.nn.sigmoid(g) * u).astype(BF16)
    acc_ref[...] += jnp.dot(act, wd_ref[...], preferred_element_type=F32)

    @pl.when(f == pl.num_programs(1) - 1)
    def _():
        out_ref[...] = x_ref[...] + _rms(acc_ref[...], gpost_ref[...])


def _ffn(x2, gain_pre, w_gate_up, w_down, layer, gain_post, *, tm, tf):
    n, d = x2.shape
    d_ff = w_down.shape[1]
    nf = d_ff // tf
    row_spec = pl.BlockSpec((tm, d), lambda i, f: (i, 0))
    vec_spec = pl.BlockSpec((1, d), lambda i, f: (0, 0))
    return pl.pallas_call(
        _ffn_kernel,
        out_shape=jax.ShapeDtypeStruct((n, d), F32),
        grid=(n // tm, nf),
        in_specs=[
            row_spec,
            vec_spec,
            pl.BlockSpec((None, d, tf), lambda i, f: (layer, 0, f)),
            pl.BlockSpec((None, d, tf), lambda i, f: (layer, 0, nf + f)),
            pl.BlockSpec((None, tf, d), lambda i, f: (layer, f, 0)),
            vec_spec,
        ],
        out_specs=row_spec,
        scratch_shapes=[pltpu.VMEM((tm, d), BF16), pltpu.VMEM((tm, d), F32)],
        compiler_params=_params(("parallel", "arbitrary")),
        name="ffn",
    )(x2, gain_pre, w_gate_up, w_gate_up, w_down, gain_post)


def _pli_kernel(x_ref, p_ref, g_ref, wg_ref, wp_ref, out_ref):
    x = x_ref[...]
    h = _rms(x, g_ref[...]).astype(BF16)
    gate = jax.nn.sigmoid(jnp.dot(h, wg_ref[...], preferred_element_type=F32))
    emb = jnp.dot(p_ref[...].astype(BF16), wp_ref[...], preferred_element_type=F32)
    out_ref[...] = x + gate * emb


def _pli(x2, p3, layer, gain, w_gate, w_proj, *, tm):
    n, d = x2.shape
    dp = p3.shape[2]
    row_spec = pl.BlockSpec((tm, d), lambda i: (i, 0))
    return pl.pallas_call(
        _pli_kernel,
        out_shape=jax.ShapeDtypeStruct((n, d), F32),
        grid=(n // tm,),
        in_specs=[
            row_spec,
            pl.BlockSpec((None, tm, dp), lambda i: (layer, i, 0)),
            pl.BlockSpec((1, d), lambda i: (0, 0)),
            pl.BlockSpec((None,) + w_gate.shape[1:], lambda i: (layer, 0, 0), pipeline_mode=pl.Buffered(1)),
            pl.BlockSpec((None,) + w_proj.shape[1:], lambda i: (layer, 0, 0), pipeline_mode=pl.Buffered(1)),
        ],
        out_specs=row_spec,
        compiler_params=_params(("parallel",)),
        name="pli",
    )(x2, p3, gain, w_gate, w_proj)


class _Tiles(NamedTuple):
    qkv_rows: int
    sb_queries: int
    sb_keys: int
    row_block: int
    ffn_cols: int


def _tiles(n, s):
    return _Tiles(qkv_rows=min(1024, n), sb_queries=min(512, s), sb_keys=256, row_block=min(512, n), ffn_cols=512)


def kernel(x, p, ln_mix_pre, w_in, ln_head, w_out, ln_mix_post, rel_bias, ln_ffn_pre, w_gate_up, w_down,
           ln_ffn_post, ln_pli, w_pli_gate, w_pli_proj):
    b, s, d = x.shape
    depth = w_in.shape[0]
    n = b * s
    assert d == N_HEADS * HEAD_DIM and w_in.shape[2] == 3 * d
    assert s % (BLOCK * DIL_GROUPS[-1][1]) == 0, "sequence must be a multiple of the widest dilated span"
    t = _tiles(n, s)

    tables = _bias_tables(rel_bias)
    x2 = x.reshape(n, d)
    p3 = p.reshape(depth, n, -1)
    w_in, w_out, w_gate_up, w_down, w_pli_gate, w_pli_proj = (
        w.astype(BF16) for w in (w_in, w_out, w_gate_up, w_down, w_pli_gate, w_pli_proj))
    for i in range(depth):
        gain_heads = ln_head[i].reshape(N_HEADS, 1, HEAD_DIM)
        qkv_sb, qkv_dl = _qkv_proj(x2, ln_mix_pre[i][None], w_in, i, tm=t.qkv_rows)
        o_sb = _sb_attention(qkv_sb.reshape(-1, b, s, HEAD_DIM), gain_heads, b, s, tq=t.sb_queries, tk=t.sb_keys)
        o_dl = _dil_attention(qkv_dl.reshape(-1, b, s, HEAD_DIM), tables, gain_heads, b, s)
        x2 = _attn_out(x2, o_sb.reshape(n, -1), o_dl.reshape(n, -1), w_out, i, ln_mix_post[i][None],
                       tm=t.row_block)
        x2 = _ffn(x2, ln_ffn_pre[i][None], w_gate_up, w_down, i, ln_ffn_post[i][None],
                  tm=t.row_block, tf=t.ffn_cols)
        x2 = _pli(x2, p3, i, ln_pli[i][None], w_pli_gate, w_pli_proj, tm=t.row_block)
    return x2.reshape(b, s, d)
```
